```python
import math
import jax, jax.numpy as jnp
from jax import lax
import numpy as np

D_MODEL = 2048
BATCH = 2
SEQ = 4096
DEPTH = 2
DEC_BATCH = 8
DEC_SEQ = 1
PAST_LEN = 16384
PAGE_SIZE = 128

N_A_LAYERS = DEPTH // 2
N_B_LAYERS = DEPTH - N_A_LAYERS
MIX_WIDTH = D_MODEL
MEM_HEADS = 4
MEM_HEAD_DIM = MIX_WIDTH // 4 // MEM_HEADS
MEM_WIDTH = MEM_HEADS * MEM_HEAD_DIM
N_MEM = 256
CONV_DIM = MIX_WIDTH - MEM_WIDTH
CONV_WIDTH = 3
DIFF_HEAD_DIM = 64
DIFF_HEADS = (MIX_WIDTH - MEM_WIDTH) // (2 * DIFF_HEAD_DIM)
QK_WIDTH = 2 * DIFF_HEADS * DIFF_HEAD_DIM
DIFF_WIDTH = DIFF_HEADS * 2 * DIFF_HEAD_DIM
D_FF = ((8 * D_MODEL // 3 + 255) // 256) * 256
ROPE_THETA = 10000.0
Q_BLOCK = 128
NORM_EPS = 1e-6

kernel_name = 'yoco_shortconv_diffattn_macaron_memory_step'


def rms_norm(x, g):
    xf = x.astype(jnp.float32)
    y = xf * lax.rsqrt(jnp.mean(xf * xf, axis=-1, keepdims=True) + NORM_EPS)
    return (y * g.astype(jnp.float32)).astype(x.dtype)


def swiglu_ffn(x, w_up, w_down):
    gu = x @ w_up
    return (jax.nn.silu(gu[..., :D_FF]) * gu[..., D_FF:]) @ w_down


def rope(x, pos):
    half = x.shape[-1] // 2
    inv_freq = ROPE_THETA ** (-jnp.arange(half, dtype=jnp.float32) / half)
    ang = pos.astype(jnp.float32)[:, None] * inv_freq[None, :]
    cos = jnp.cos(ang)[None, :, None, :]
    sin = jnp.sin(ang)[None, :, None, :]
    xf = x.astype(jnp.float32)
    x1, x2 = xf[..., :half], xf[..., half:]
    return jnp.concatenate([x1 * cos - x2 * sin, x2 * cos + x1 * sin], axis=-1).astype(x.dtype)


def causal_depthwise_conv(u_ext, w):
    s_len = u_ext.shape[1] - (CONV_WIDTH - 1)
    out = w[0] * u_ext[:, 0:s_len]
    for j in range(1, CONV_WIDTH):
        out = out + w[j] * u_ext[:, j:j + s_len]
    return out


def memory_attention(q, mem_k, mem_v):
    s = jnp.einsum('bshd,bmhd->bhsm', q, mem_k, preferred_element_type=jnp.float32) * (MEM_HEAD_DIM ** -0.5)
    p = jax.nn.softmax(s, axis=-1).astype(mem_v.dtype)
    o = jnp.einsum('bhsm,bmhd->bshd', p, mem_v)
    return o.reshape(q.shape[0], q.shape[1], MEM_WIDTH)


def shared_kv(x, pos, norm_kv, w_kv):
    bsz, s_len = x.shape[0], x.shape[1]
    kv = rms_norm(x, norm_kv) @ w_kv
    k = rope(kv[..., :QK_WIDTH].reshape(bsz, s_len, 2 * DIFF_HEADS, DIFF_HEAD_DIM), pos)
    v = kv[..., QK_WIDTH:].reshape(bsz, s_len, DIFF_HEADS, 2 * DIFF_HEAD_DIM)
    return k, v


def diff_attention_core(q, k, v, q_pos, k_pos, lam):
    bsz, nq = q.shape[0], q.shape[1]
    nk = k.shape[1]
    s = jnp.einsum('bqhd,bkhd->bhqk', q, k, preferred_element_type=jnp.float32) * (DIFF_HEAD_DIM ** -0.5)
    s = jnp.where(k_pos[None, :] <= q_pos[:, None], s, -jnp.inf)
    a = jax.nn.softmax(s, axis=-1).reshape(bsz, DIFF_HEADS, 2, nq, nk)
    w = a[:, :, 0] - lam * a[:, :, 1]
    return jnp.einsum('bhqk,bkhe->bqhe', w.astype(v.dtype), v)


def diff_attention_blocked(q, k, v, lam):
    bsz, s_len = q.shape[0], q.shape[1]
    n_blk = s_len // Q_BLOCK
    k_pos = jnp.arange(s_len)
    q_blocks = jnp.moveaxis(q.reshape(bsz, n_blk, Q_BLOCK, 2 * DIFF_HEADS, DIFF_HEAD_DIM), 1, 0)
    starts = jnp.arange(n_blk) * Q_BLOCK

    def one_block(args):
        q_blk, start = args
        return diff_attention_core(q_blk, k, v, start + jnp.arange(Q_BLOCK), k_pos, lam)

    out = lax.map(one_block, (q_blocks, starts))
    return jnp.moveaxis(out, 0, 1).reshape(bsz, s_len, DIFF_HEADS, 2 * DIFF_HEAD_DIM)


def trunk(x, pos, conv_prev, mem_k, mem_v, past_k, past_v, p):
    bsz, s_len = x.shape[0], x.shape[1]
    new_conv = []
    k_sh, v_sh = None, None
    for l in range(DEPTH):
        if l == N_A_LAYERS:
            k_sh, v_sh = shared_kv(x, pos, p['norm_kv'], p['w_kv'])
        x = x + 0.5 * swiglu_ffn(rms_norm(x, p['norm_ffn1'][l]), p['w_ffn1_up'][l], p['w_ffn1_down'][l])
        h = rms_norm(x, p['norm_mix'][l])
        if l < N_A_LAYERS:
            proj = h @ p['w_in_a'][l]
            b_gate = proj[..., :CONV_DIM]
            c_gate = proj[..., CONV_DIM:2 * CONV_DIM]
            x_in = proj[..., 2 * CONV_DIM:3 * CONV_DIM]
            q_mem = proj[..., 3 * CONV_DIM:]
            u_ext = jnp.concatenate([conv_prev[l], c_gate * x_in], axis=1)
            new_conv.append(u_ext[:, -(CONV_WIDTH - 1):])
            mix = b_gate * causal_depthwise_conv(u_ext, p['conv_w'][l])
        else:
            j = l - N_A_LAYERS
            proj = h @ p['w_in_b'][j]
            q = rope(proj[..., :QK_WIDTH].reshape(bsz, s_len, 2 * DIFF_HEADS, DIFF_HEAD_DIM), pos)
            q_mem = proj[..., QK_WIDTH:]
            lam_init = 0.8 - 0.6 * math.exp(-0.3 * l)
            lam = (jnp.exp(jnp.sum(p['lambda_q1'][j].astype(jnp.float32) * p['lambda_k1'][j].astype(jnp.float32)))
                   - jnp.exp(jnp.sum(p['lambda_q2'][j].astype(jnp.float32) * p['lambda_k2'][j].astype(jnp.float32)))
                   + lam_init)
            if past_k is None:
                o = diff_attention_blocked(q, k_sh, v_sh, lam)
            else:
                k_all = jnp.concatenate([past_k, k_sh], axis=1)
                v_all = jnp.concatenate([past_v, v_sh], axis=1)
                o = diff_attention_core(q, k_all, v_all, pos, jnp.arange(k_all.shape[1]), lam)
            o = rms_norm(o, p['subln_gain'][j]) * (1.0 - lam_init)
            mix = o.reshape(bsz, s_len, DIFF_WIDTH)
        mem_o = memory_attention(q_mem.reshape(bsz, s_len, MEM_HEADS, MEM_HEAD_DIM), mem_k[l], mem_v[l])
        x = x + jnp.concatenate([mix, mem_o], axis=-1) @ p['w_out'][l]
        x = x + 0.5 * swiglu_ffn(rms_norm(x, p['norm_ffn2'][l]), p['w_ffn2_up'][l], p['w_ffn2_down'][l])
    return rms_norm(x, p['norm_final']), jnp.stack(new_conv, axis=0), k_sh, v_sh


def setup_inputs(seed: int = 0) -> dict:
    key = jax.random.key(seed)
    ks = list(jax.random.split(key, 40))
    f32 = jnp.float32

    def nxt():
        return ks.pop()

    def w(shape, fan_in):
        return jax.random.normal(nxt(), shape, f32) * (fan_in ** -0.5)

    def gain(shape):
        return 1.0 + 0.02 * jax.random.normal(nxt(), shape, f32)

    n_pages = PAST_LEN // PAGE_SIZE
    n_used = DEC_BATCH * n_pages
    n_pool = n_used + max(1, n_used // 4)
    page_table = jax.random.permutation(nxt(), n_pool)[:n_used].reshape(DEC_BATCH, n_pages).astype(jnp.int32)
    return {
        'x_prompt': jax.random.normal(nxt(), (BATCH, SEQ, D_MODEL), f32),
        'x_sample': jax.random.normal(nxt(), (DEC_BATCH, DEC_SEQ, D_MODEL), f32),
        'state_conv': jax.random.normal(nxt(), (N_A_LAYERS, DEC_BATCH, CONV_WIDTH - 1, CONV_DIM), f32),
        'cache_k': jax.random.normal(nxt(), (n_pool, PAGE_SIZE, 2 * DIFF_HEADS, DIFF_HEAD_DIM), f32),
        'cache_v': jax.random.normal(nxt(), (n_pool, PAGE_SIZE, DIFF_HEADS, 2 * DIFF_HEAD_DIM), f32),
        'cache_mem_k': jax.random.normal(nxt(), (DEPTH, DEC_BATCH, N_MEM, MEM_HEADS, MEM_HEAD_DIM), f32),
        'cache_mem_v': jax.random.normal(nxt(), (DEPTH, DEC_BATCH, N_MEM, MEM_HEADS, MEM_HEAD_DIM), f32),
        'page_table': page_table,
        'mem_prompt': jax.random.normal(nxt(), (BATCH, N_MEM, D_MODEL), f32),
        'norm_ffn1': gain((DEPTH, D_MODEL)),
        'w_ffn1_up': w((DEPTH, D_MODEL, 2 * D_FF), D_MODEL),
        'w_ffn1_down': w((DEPTH, D_FF, D_MODEL), D_FF),
        'norm_mix': gain((DEPTH, D_MODEL)),
        'w_in_a': w((N_A_LAYERS, D_MODEL, 3 * CONV_DIM + MEM_WIDTH), D_MODEL),
        'conv_w': w((N_A_LAYERS, CONV_WIDTH, CONV_DIM), CONV_WIDTH),
        'w_in_b': w((N_B_LAYERS, D_MODEL, QK_WIDTH + MEM_WIDTH), D_MODEL),
        'lambda_q1': 0.1 * jax.random.normal(nxt(), (N_B_LAYERS, DIFF_HEAD_DIM), f32),
        'lambda_k1': 0.1 * jax.random.normal(nxt(), (N_B_LAYERS, DIFF_HEAD_DIM), f32),
        'lambda_q2': 0.1 * jax.random.normal(nxt(), (N_B_LAYERS, DIFF_HEAD_DIM), f32),
        'lambda_k2': 0.1 * jax.random.normal(nxt(), (N_B_LAYERS, DIFF_HEAD_DIM), f32),
        'subln_gain': gain((N_B_LAYERS, 2 * DIFF_HEAD_DIM)),
        'norm_mem': gain((DEPTH, D_MODEL)),
        'w_mem_kv': w((DEPTH, D_MODEL, 2 * MEM_WIDTH), D_MODEL),
        'w_out': w((DEPTH, MIX_WIDTH, D_MODEL), MIX_WIDTH),
        'norm_ffn2': gain((DEPTH, D_MODEL)),
        'w_ffn2_up': w((DEPTH, D_MODEL, 2 * D_FF), D_MODEL),
        'w_ffn2_down': w((DEPTH, D_FF, D_MODEL), D_FF),
        'norm_kv': gain((D_MODEL,)),
        'w_kv': w((D_MODEL, QK_WIDTH + DIFF_WIDTH), D_MODEL),
        'norm_final': gain((D_MODEL,)),
    }


def reference(x_prompt, x_sample, state_conv, cache_k, cache_v, cache_mem_k, cache_mem_v, page_table,
              mem_prompt, norm_ffn1, w_ffn1_up, w_ffn1_down, norm_mix, w_in_a, conv_w, w_in_b,
              lambda_q1, lambda_k1, lambda_q2, lambda_k2, subln_gain, norm_mem, w_mem_kv, w_out,
              norm_ffn2, w_ffn2_up, w_ffn2_down, norm_kv, w_kv, norm_final):
    p = {
        'norm_ffn1': norm_ffn1, 'w_ffn1_up': w_ffn1_up, 'w_ffn1_down': w_ffn1_down,
        'norm_mix': norm_mix, 'w_in_a': w_in_a, 'conv_w': conv_w, 'w_in_b': w_in_b,
        'lambda_q1': lambda_q1, 'lambda_k1': lambda_k1, 'lambda_q2': lambda_q2, 'lambda_k2': lambda_k2,
        'subln_gain': subln_gain, 'w_out': w_out,
        'norm_ffn2': norm_ffn2, 'w_ffn2_up': w_ffn2_up, 'w_ffn2_down': w_ffn2_down,
        'norm_kv': norm_kv, 'w_kv': w_kv, 'norm_final': norm_final,
    }

    bsz_p, s_p = x_prompt.shape[0], x_prompt.shape[1]
    n_mem = mem_prompt.shape[1]
    mem_k_list, mem_v_list = [], []
    for l in range(DEPTH):
        kv = rms_norm(mem_prompt, norm_mem[l]) @ w_mem_kv[l]
        mem_k_list.append(kv[..., :MEM_WIDTH].reshape(bsz_p, n_mem, MEM_HEADS, MEM_HEAD_DIM))
        mem_v_list.append(kv[..., MEM_WIDTH:].reshape(bsz_p, n_mem, MEM_HEADS, MEM_HEAD_DIM))
    mem_k_prompt = jnp.stack(mem_k_list, axis=0)
    mem_v_prompt = jnp.stack(mem_v_list, axis=0)
    conv_zero = jnp.zeros((N_A_LAYERS, bsz_p, CONV_WIDTH - 1, CONV_DIM), x_prompt.dtype)
    y_prompt, conv_prompt, k_prompt, v_prompt = trunk(
        x_prompt, jnp.arange(s_p), conv_zero, mem_k_prompt, mem_v_prompt, None, None, p)

    bsz_s, s_s = x_sample.shape[0], x_sample.shape[1]
    n_pages = page_table.shape[1]
    past_len = n_pages * PAGE_SIZE
    past_k = cache_k[page_table].reshape(bsz_s, past_len, 2 * DIFF_HEADS, DIFF_HEAD_DIM)
    past_v = cache_v[page_table].reshape(bsz_s, past_len, DIFF_HEADS, 2 * DIFF_HEAD_DIM)
    y_sample, conv_sample, k_sample, v_sample = trunk(
        x_sample, past_len + jnp.arange(s_s), state_conv, cache_mem_k, cache_mem_v, past_k, past_v, p)

    return (y_prompt, y_sample, conv_prompt, conv_sample, k_prompt, v_prompt, k_sample, v_sample,
            mem_k_prompt, mem_v_prompt)
```

```python
import functools
import math

import jax
import jax.numpy as jnp
from jax import lax
from jax.experimental import pallas as pl
from jax.experimental.pallas import tpu as pltpu

F32 = jnp.float32
BF16 = jnp.bfloat16

NORM_EPS = 1e-6
ROPE_THETA = 10000.0
LANES = 128
VMEM_LIMIT = 48 * 1024 * 1024

MEM_HEADS = 4
MEM_HEAD_DIM = 128
DIFF_HEAD_DIM = 64
CONV_WIDTH = 3
DECODE_PAGES_PER_STEP = 8
DECODE_ROWS = 32


def _params(*sem):
    return pltpu.CompilerParams(dimension_semantics=sem, vmem_limit_bytes=VMEM_LIMIT)


def _rms(x, g):
    ms = jnp.mean(x * x, axis=-1, keepdims=True)
    return x * lax.rsqrt(ms + NORM_EPS) * g


def _rope_tile(y, cos, sin_lo, sin_hi):
    outs = []
    for c in range(y.shape[1] // LANES):
        yc = y[:, c * LANES:(c + 1) * LANES]
        up = pltpu.roll(yc, LANES - 32, axis=1)
        dn = pltpu.roll(yc, 32, axis=1)
        outs.append(yc * cos + up * sin_lo + dn * sin_hi)
    return jnp.concatenate(outs, axis=1) if len(outs) > 1 else outs[0]


def _norm_matmul_kernel(*refs, n_out, n_rope_tiles, scale):
    if n_rope_tiles:
        x_ref, g_ref, w_ref, cos_ref, slo_ref, shi_ref = refs[:6]
        rest = refs[6:]
    else:
        x_ref, g_ref, w_ref = refs[:3]
        rest = refs[3:]
    out_refs, h_ref = rest[:n_out], rest[n_out]
    j = pl.program_id(1)

    @pl.when(j == 0)
    def _():
        h_ref[...] = _rms(x_ref[...], g_ref[...]).astype(BF16)

    y = jnp.dot(h_ref[...], w_ref[...], preferred_element_type=F32)

    def store(v):
        for o in out_refs:
            o[...] = v.astype(o.dtype)

    if n_rope_tiles:
        @pl.when(j < n_rope_tiles)
        def _():
            store(_rope_tile(y, cos_ref[...], slo_ref[...], shi_ref[...]) * scale)

        @pl.when(j >= n_rope_tiles)
        def _():
            store(y)
    else:
        store(y)


def _norm_matmul(x, g, w, *, tm, tn, col_off=0, n_cols=None, out_dtypes=(F32,),
                 rope=None, rope_cols=0, scale=1.0):
    m, d = x.shape
    n_cols = w.shape[1] - col_off if n_cols is None else n_cols
    assert m % tm == 0 and n_cols % tn == 0 and col_off % tn == 0 and rope_cols % tn == 0
    off_blocks = col_off // tn
    in_specs = [
        pl.BlockSpec((tm, d), lambda i, j: (i, 0)),
        pl.BlockSpec((1, d), lambda i, j: (0, 0)),
        pl.BlockSpec((d, tn), lambda i, j: (0, off_blocks + j)),
    ]
    args = [x, g.reshape(1, d), w]
    if rope_cols:
        cos, slo, shi = rope
        n_tab = cos.shape[0] // tm
        for t in (cos, slo, shi):
            in_specs.append(pl.BlockSpec((tm, LANES), lambda i, j: (i % n_tab, 0)))
            args.append(t)
    outs = pl.pallas_call(
        functools.partial(_norm_matmul_kernel, n_out=len(out_dtypes),
                          n_rope_tiles=rope_cols // tn, scale=scale),
        grid=(m // tm, n_cols // tn),
        in_specs=in_specs,
        out_specs=[pl.BlockSpec((tm, tn), lambda i, j: (i, j)) for _ in out_dtypes],
        out_shape=[jax.ShapeDtypeStruct((m, n_cols), dt) for dt in out_dtypes],
        scratch_shapes=[pltpu.VMEM((tm, d), BF16)],
        compiler_params=_params("parallel", "arbitrary"),
    )(*args)
    return outs


def _ffn_kernel(*refs, final_norm):
    if final_norm:
        x_ref, g_ref, wg_ref, wu_ref, wd_ref, gf_ref, o_ref, h_ref = refs
    else:
        x_ref, g_ref, wg_ref, wu_ref, wd_ref, o_ref, h_ref = refs
    j = pl.program_id(1)

    @pl.when(j == 0)
    def _():
        x = x_ref[...]
        h_ref[...] = _rms(x, g_ref[...]).astype(BF16)
        o_ref[...] = x

    h = h_ref[...]
    gate = jnp.dot(h, wg_ref[...], preferred_element_type=F32)
    up = jnp.dot(h, wu_ref[...], preferred_element_type=F32)
    act = (gate * jax.nn.sigmoid(gate) * up * 0.5).astype(BF16)
    o_ref[...] += jnp.dot(act, wd_ref[...], preferred_element_type=F32)

    if final_norm:
        @pl.when(j == pl.num_programs(1) - 1)
        def _():
            o_ref[...] = _rms(o_ref[...], gf_ref[...])


def _ffn(x, g, w_up, w_down, *, tm, tf, final_gain=None):
    m, d = x.shape
    f = w_down.shape[0]
    assert m % tm == 0 and f % tf == 0 and w_up.shape[1] == 2 * f
    nf = f // tf
    in_specs = [
        pl.BlockSpec((tm, d), lambda i, j: (i, 0)),
        pl.BlockSpec((1, d), lambda i, j: (0, 0)),
        pl.BlockSpec((d, tf), lambda i, j: (0, j)),
        pl.BlockSpec((d, tf), lambda i, j: (0, nf + j)),
        pl.BlockSpec((tf, d), lambda i, j: (j, 0)),
    ]
    args = [x, g.reshape(1, d), w_up, w_up, w_down]
    if final_gain is not None:
        in_specs.append(pl.BlockSpec((1, d), lambda i, j: (0, 0)))
        args.append(final_gain.reshape(1, d))
    return pl.pallas_call(
        functools.partial(_ffn_kernel, final_norm=final_gain is not None),
        grid=(m // tm, nf),
        in_specs=in_specs,
        out_specs=pl.BlockSpec((tm, d), lambda i, j: (i, 0)),
        out_shape=jax.ShapeDtypeStruct((m, d), F32),
        scratch_shapes=[pltpu.VMEM((tm, d), BF16)],
        compiler_params=_params("parallel", "arbitrary"),
    )(*args)


def _conv_gate_kernel(b_ref, c_ref, xin_ref, w_ref, mix_ref, st_ref, ubuf, *, tiles_per_seq):
    tm = b_ref.shape[0]
    i = pl.program_id(0)

    @pl.when(i % tiles_per_seq == 0)
    def _():
        ubuf[0:8, :] = jnp.zeros((8, ubuf.shape[1]), F32)

    u = c_ref[...] * xin_ref[...]
    ubuf[8:tm + 8, :] = u
    w = w_ref[...]
    conv = w[0:1] * ubuf[6:tm + 6, :] + w[1:2] * ubuf[7:tm + 7, :] + w[2:3] * u
    mix_ref[...] = (b_ref[...] * conv).astype(mix_ref.dtype)
    st_ref[...] = u[tm - (CONV_WIDTH - 1):tm]
    ubuf[0:8, :] = u[tm - 8:tm]


def _conv_gate_prompt(proj, conv_w, *, bsz, tm, cdim):
    m = proj.shape[0]
    tiles_per_seq = m // bsz // tm
    col = lambda cb: pl.BlockSpec((tm, cdim), lambda i: (i, cb))
    return pl.pallas_call(
        functools.partial(_conv_gate_kernel, tiles_per_seq=tiles_per_seq),
        grid=(m // tm,),
        in_specs=[col(0), col(1), col(2), pl.BlockSpec((CONV_WIDTH, cdim), lambda i: (0, 0))],
        out_specs=[pl.BlockSpec((tm, cdim), lambda i: (i, 0)),
                   pl.BlockSpec((None, CONV_WIDTH - 1, cdim), lambda i: (i // tiles_per_seq, 0, 0))],
        out_shape=[jax.ShapeDtypeStruct((m, cdim), BF16),
                   jax.ShapeDtypeStruct((bsz, CONV_WIDTH - 1, cdim), F32)],
        scratch_shapes=[pltpu.VMEM((tm + 8, cdim), F32)],
        compiler_params=_params("arbitrary"),
    )(proj, proj, proj, conv_w)


def _conv_gate_step_kernel(b_ref, c_ref, xin_ref, s0_ref, s1_ref, w_ref, mix_ref, u_ref):
    u = c_ref[...] * xin_ref[...]
    w = w_ref[...]
    conv = w[0:1] * s0_ref[...] + w[1:2] * s1_ref[...] + w[2:3] * u
    mix_ref[...] = (b_ref[...] * conv).astype(mix_ref.dtype)
    u_ref[...] = u


def _conv_gate_step(proj, s0, s1, conv_w, *, cdim):
    m = proj.shape[0]
    col = lambda cb: pl.BlockSpec((m, cdim), lambda i: (0, cb))
    full = lambda r: pl.BlockSpec((r, cdim), lambda i: (0, 0))
    return pl.pallas_call(
        _conv_gate_step_kernel,
        grid=(1,),
        in_specs=[col(0), col(1), col(2), full(m), full(m), full(CONV_WIDTH)],
        out_specs=[full(m), full(m)],
        out_shape=[jax.ShapeDtypeStruct((m, cdim), BF16), jax.ShapeDtypeStruct((m, cdim), F32)],
        compiler_params=_params("arbitrary"),
    )(proj, proj, proj, s0, s1, conv_w)


def _mem_attn_kernel(q_ref, k_ref, v_ref, o_ref):
    rows = q_ref.shape[0]
    q = q_ref[...].astype(BF16)
    if rows < 8:
        q = jnp.broadcast_to(q[0:1], (8, q.shape[1]))
    k = k_ref[...].astype(BF16)
    v = v_ref[...].astype(BF16)
    scale = MEM_HEAD_DIM ** -0.5
    for h in range(MEM_HEADS):
        hs = slice(h * MEM_HEAD_DIM, (h + 1) * MEM_HEAD_DIM)
        s = lax.dot_general(q[:, hs], k[:, hs], (((1,), (1,)), ((), ())),
                            preferred_element_type=F32) * scale
        p = jnp.exp(s - jnp.max(s, axis=-1, keepdims=True))
        l = jnp.sum(p, axis=-1, keepdims=True)
        o = jnp.dot(p.astype(BF16), v[:, hs], preferred_element_type=F32) / l
        o_ref[:, hs] = o[:rows].astype(o_ref.dtype)


def _mem_attn(q3, q_col_block, k3, k_col_block, v3, v_col_block, *, tq):
    nb, rows, _ = q3.shape
    n_mem = k3.shape[1]
    width = MEM_HEADS * MEM_HEAD_DIM
    return pl.pallas_call(
        _mem_attn_kernel,
        grid=(nb, rows // tq),
        in_specs=[pl.BlockSpec((None, tq, width), lambda b, i: (b, i, q_col_block)),
                  pl.BlockSpec((None, n_mem, width), lambda b, i: (b, 0, k_col_block)),
                  pl.BlockSpec((None, n_mem, width), lambda b, i: (b, 0, v_col_block))],
        out_specs=pl.BlockSpec((None, tq, width), lambda b, i: (b, i, 0)),
        out_shape=jax.ShapeDtypeStruct((nb, rows, width), BF16),
        compiler_params=_params("parallel", "parallel"),
    )(q3, k3, v3)


def _out_proj_kernel(a1_ref, a2_ref, w_ref, x_ref, o_ref):
    n1 = a1_ref.shape[1]
    acc = jnp.dot(a1_ref[...], w_ref[0:n1, :], preferred_element_type=F32)
    acc = acc + jnp.dot(a2_ref[...], w_ref[n1:, :], preferred_element_type=F32)
    o_ref[...] = x_ref[...] + acc


def _out_proj(a1, a2, w, x, *, tm):
    m, d = x.shape
    n1, n2 = a1.shape[1], a2.shape[1]
    return pl.pallas_call(
        _out_proj_kernel,
        grid=(m // tm,),
        in_specs=[pl.BlockSpec((tm, n1), lambda i: (i, 0)),
                  pl.BlockSpec((tm, n2), lambda i: (i, 0)),
                  pl.BlockSpec((n1 + n2, d), lambda i: (0, 0)),
                  pl.BlockSpec((tm, d), lambda i: (i, 0))],
        out_specs=pl.BlockSpec((tm, d), lambda i: (i, 0)),
        out_shape=jax.ShapeDtypeStruct((m, d), F32),
        compiler_params=_params("parallel"),
    )(a1, a2, w, x)


def _lambda_value(lq1_ref, lk1_ref, lq2_ref, lk2_ref, lam_init):
    a = jnp.exp(jnp.sum(lq1_ref[...] * lk1_ref[...], axis=-1, keepdims=True))
    b = jnp.exp(jnp.sum(lq2_ref[...] * lk2_ref[...], axis=-1, keepdims=True))
    return a - b + lam_init


def _subln(d, gain, lam_init):
    return _rms(d, gain) * (1.0 - lam_init)


def _diff_attn_kernel(q_ref, k_ref, v_ref, lq1_ref, lk1_ref, lq2_ref, lk2_ref, g_ref, o_ref,
                      q2_ref, m_ref, l_ref, acc_ref, *, lam_init):
    tq = q_ref.shape[0]
    tk = tq
    qi = pl.program_id(2)
    q = q_ref[...]
    lane = lax.broadcasted_iota(jnp.int32, q.shape, 1)
    zero = jnp.zeros_like(q)
    q2_ref[0:tq, :] = jnp.where(lane < DIFF_HEAD_DIM, q, zero)
    q2_ref[tq:2 * tq, :] = jnp.where(lane < DIFF_HEAD_DIM, zero, q)
    m_ref[...] = jnp.full(m_ref.shape, -jnp.inf, F32)
    l_ref[...] = jnp.zeros(l_ref.shape, F32)
    acc_ref[...] = jnp.zeros(acc_ref.shape, F32)

    def block(kj, masked):
        start = pl.multiple_of(kj * tk, tk)
        k = k_ref[pl.ds(start, tk), :]
        v = v_ref[pl.ds(start, tk), :]
        s = lax.dot_general(q2_ref[...], k, (((1,), (1,)), ((), ())), preferred_element_type=F32)
        if masked:
            row = lax.broadcasted_iota(jnp.int32, s.shape, 0)
            col = lax.broadcasted_iota(jnp.int32, s.shape, 1)
            row = jnp.where(row >= tq, row - tq, row)
            s = jnp.where(col <= row, s, -jnp.inf)
        m_prev = m_ref[:, 0:1]
        m_next = jnp.maximum(m_prev, jnp.max(s, axis=-1, keepdims=True))
        alpha = jnp.exp(m_prev - m_next)
        p = jnp.exp(s - m_next)
        l_next = alpha * l_ref[:, 0:1] + jnp.sum(p, axis=-1, keepdims=True)
        acc_ref[...] = alpha * acc_ref[...] + jnp.dot(p.astype(BF16), v, preferred_element_type=F32)
        m_ref[...] = jnp.broadcast_to(m_next, m_ref.shape)
        l_ref[...] = jnp.broadcast_to(l_next, l_ref.shape)

    def body(kj, carry):
        block(kj, False)
        return carry

    lax.fori_loop(0, qi, body, 0)
    block(qi, True)

    o = acc_ref[...] / l_ref[:, 0:1]
    lam = _lambda_value(lq1_ref, lk1_ref, lq2_ref, lk2_ref, lam_init)
    d = o[0:tq] - lam * o[tq:2 * tq]
    o_ref[...] = _subln(d, g_ref[...], lam_init).astype(o_ref.dtype)


def _diff_attn_prompt(q, k, v, lam_params, gain, *, bsz, seq, tq, lam_init):
    m = bsz * seq
    width = k.shape[1]
    heads = width // LANES
    nq = seq // tq
    small = lambda a: pl.BlockSpec(a.shape, lambda b, h, i: (0, 0))
    lam_args = [a.reshape(1, -1) for a in lam_params]
    gain = gain.reshape(1, -1)
    return pl.pallas_call(
        functools.partial(_diff_attn_kernel, lam_init=lam_init),
        grid=(bsz, heads, nq),
        in_specs=[pl.BlockSpec((tq, LANES), lambda b, h, i: (b * nq + i, h)),
                  pl.BlockSpec((seq, LANES), lambda b, h, i: (b, h)),
                  pl.BlockSpec((seq, LANES), lambda b, h, i: (b, h))]
                 + [small(a) for a in lam_args] + [small(gain)],
        out_specs=pl.BlockSpec((tq, LANES), lambda b, h, i: (b * nq + i, h)),
        out_shape=jax.ShapeDtypeStruct((m, width), BF16),
        scratch_shapes=[pltpu.VMEM((2 * tq, LANES), BF16),
                        pltpu.VMEM((2 * tq, LANES), F32),
                        pltpu.VMEM((2 * tq, LANES), F32),
                        pltpu.VMEM((2 * tq, LANES), F32)],
        compiler_params=_params("parallel", "parallel", "arbitrary"),
    )(q, k, v, *lam_args, gain)


def _decode_attn_kernel(pt_ref, q_ref, kn_ref, vn_ref, *refs, n_pages, lam_init):
    del pt_ref
    k_refs, v_refs = refs[:n_pages], refs[n_pages:2 * n_pages]
    lq1_ref, lk1_ref, lq2_ref, lk2_ref, g_ref, o_ref, qbd_ref, m_ref, l_ref, acc_ref = refs[2 * n_pages:]
    g = pl.program_id(1)
    width = q_ref.shape[1]

    @pl.when(g == 0)
    def _():
        row = lax.broadcasted_iota(jnp.int32, (DECODE_ROWS, width), 0)
        lane = lax.broadcasted_iota(jnp.int32, (DECODE_ROWS, width), 1)
        qb = jnp.broadcast_to(q_ref[...], (DECODE_ROWS, width))
        keep = (lane >= row * DIFF_HEAD_DIM) & (lane < (row + 1) * DIFF_HEAD_DIM)
        qbd_ref[...] = jnp.where(keep, qb, 0.0).astype(BF16)
        m_ref[...] = jnp.full(m_ref.shape, -jnp.inf, F32)
        l_ref[...] = jnp.zeros(l_ref.shape, F32)
        acc_ref[...] = jnp.zeros(acc_ref.shape, F32)

    qbd = qbd_ref[...]
    s = jnp.concatenate(
        [jnp.dot(qbd, kr[...].astype(BF16), preferred_element_type=F32) for kr in k_refs], axis=1)
    m_prev = m_ref[:, 0:1]
    m_next = jnp.maximum(m_prev, jnp.max(s, axis=-1, keepdims=True))
    alpha = jnp.exp(m_prev - m_next)
    p = jnp.exp(s - m_next)
    l_next = alpha * l_ref[:, 0:1] + jnp.sum(p, axis=-1, keepdims=True)
    pb = p.astype(BF16)
    page = k_refs[0].shape[1]
    for h in range(width // LANES):
        hs = slice(h * LANES, (h + 1) * LANES)
        pv = jnp.dot(pb[:, 0:page], v_refs[0][h].astype(BF16), preferred_element_type=F32)
        for r in range(1, n_pages):
            pv = pv + jnp.dot(pb[:, r * page:(r + 1) * page], v_refs[r][h].astype(BF16),
                              preferred_element_type=F32)
        acc_ref[:, hs] = alpha * acc_ref[:, hs] + pv
    m_ref[...] = jnp.broadcast_to(m_next, m_ref.shape)
    l_ref[...] = jnp.broadcast_to(l_next, l_ref.shape)

    @pl.when(g == pl.num_programs(1) - 1)
    def _():
        kn = kn_ref[...].astype(BF16).astype(F32)
        s_new = jnp.sum(qbd_ref[...].astype(F32) * kn, axis=-1, keepdims=True)
        m_old = m_ref[:, 0:1]
        m_fin = jnp.maximum(m_old, s_new)
        a = jnp.exp(m_old - m_fin)
        p_new = jnp.exp(s_new - m_fin)
        l_fin = a * l_ref[:, 0:1] + p_new
        o = (a * acc_ref[...] + p_new * vn_ref[...]) / l_fin
        lam = _lambda_value(lq1_ref, lk1_ref, lq2_ref, lk2_ref, lam_init)
        gain = g_ref[...]
        for h in range(width // LANES):
            hs = slice(h * LANES, (h + 1) * LANES)
            d = o[2 * h:2 * h + 1, hs] - lam * o[2 * h + 1:2 * h + 2, hs]
            o_ref[:, hs] = _subln(d, gain, lam_init).astype(o_ref.dtype)


def _decode_attn(q, k_new, v_new, cache_k, cache_v, page_table, lam_params, gain, *, lam_init):
    nb, _, width = q.shape
    page = cache_k.shape[2]
    n_used = page_table.shape[1]
    npg = DECODE_PAGES_PER_STEP
    assert n_used % npg == 0
    row = pl.BlockSpec((None, 1, width), lambda b, g, pt: (b, 0, 0))
    small = lambda a: pl.BlockSpec(a.shape, lambda b, g, pt: (0, 0))
    k_spec = lambda r: pl.BlockSpec((None, width, page),
                                    lambda b, g, pt: (pt[b, g * npg + r], 0, 0))
    v_spec = lambda r: pl.BlockSpec((None, width // LANES, page, LANES),
                                    lambda b, g, pt: (pt[b, g * npg + r], 0, 0, 0))
    lam_args = [a.reshape(1, -1) for a in lam_params]
    gain = gain.reshape(1, -1)
    grid_spec = pltpu.PrefetchScalarGridSpec(
        num_scalar_prefetch=1,
        grid=(nb, n_used // npg),
        in_specs=[row, row, row] + [k_spec(r) for r in range(npg)] + [v_spec(r) for r in range(npg)]
                 + [small(a) for a in lam_args] + [small(gain)],
        out_specs=pl.BlockSpec((None, 1, width), lambda b, g, pt: (b, 0, 0)),
        scratch_shapes=[pltpu.VMEM((DECODE_ROWS, width), BF16),
                        pltpu.VMEM((DECODE_ROWS, LANES), F32),
                        pltpu.VMEM((DECODE_ROWS, LANES), F32),
                        pltpu.VMEM((DECODE_ROWS, width), F32)],
    )
    return pl.pallas_call(
        functools.partial(_decode_attn_kernel, n_pages=npg, lam_init=lam_init),
        grid_spec=grid_spec,
        out_shape=jax.ShapeDtypeStruct((nb, 1, width), BF16),
        compiler_params=_params("parallel", "arbitrary"),
    )(page_table, q, k_new, v_new, *([cache_k] * npg), *([cache_v] * npg), *lam_args, gain)


def _rope_tables(pos):
    half = DIFF_HEAD_DIM // 2
    inv_freq = ROPE_THETA ** (-jnp.arange(half, dtype=F32) / half)
    ang = pos.astype(F32)[:, None] * inv_freq[None, :]
    cos, sin, zero = jnp.cos(ang), jnp.sin(ang), jnp.zeros_like(ang)
    reps = LANES // DIFF_HEAD_DIM
    cos_t = jnp.tile(jnp.concatenate([cos, cos], axis=1), (1, reps))
    sin_lo = jnp.tile(jnp.concatenate([-sin, zero], axis=1), (1, reps))
    sin_hi = jnp.tile(jnp.concatenate([zero, sin], axis=1), (1, reps))
    return cos_t, sin_lo, sin_hi


def _trunk(x, p, *, bsz, seq, tm, rope, mem_k, mem_v, conv_state, cache, depth):
    m, d = x.shape
    n_a = depth // 2
    tf = 512
    tn = 512
    cdim = p['conv_w'].shape[-1]
    qk_width = p['w_kv'].shape[1] // 2
    tq_mem = min(tm, seq)
    new_conv = []
    k_f32 = v_f32 = k_b = v_b = None
    for l in range(depth):
        if l == n_a:
            k_f32, k_b = _norm_matmul(x, p['norm_kv'], p['w_kv'], tm=tm, tn=tn, n_cols=qk_width,
                                      out_dtypes=(F32, BF16), rope=rope, rope_cols=qk_width)
            v_f32, v_b = _norm_matmul(x, p['norm_kv'], p['w_kv'], tm=tm, tn=tn, col_off=qk_width,
                                      out_dtypes=(F32, BF16))
        x = _ffn(x, p['norm_ffn1'][l], p['w_ffn1_up'][l], p['w_ffn1_down'][l], tm=tm, tf=tf)
        if l < n_a:
            proj, = _norm_matmul(x, p['norm_mix'][l], p['w_in_a'][l], tm=tm, tn=tn)
            if conv_state is None:
                mix, st = _conv_gate_prompt(proj, p['conv_w'][l], bsz=bsz, tm=tm, cdim=cdim)
            else:
                s0, s1 = conv_state[l]
                mix, u = _conv_gate_step(proj, s0, s1, p['conv_w'][l], cdim=cdim)
                st = jnp.stack([s1, u], axis=1)
            new_conv.append(st)
            q_cols = 3 * cdim
        else:
            j = l - n_a
            lam_init = 0.8 - 0.6 * math.exp(-0.3 * l)
            lam_params = (p['lambda_q1'][j], p['lambda_k1'][j], p['lambda_q2'][j], p['lambda_k2'][j])
            q_dtype = BF16 if cache is None else F32
            proj, = _norm_matmul(x, p['norm_mix'][l], p['w_in_b'][j], tm=tm, tn=tn, out_dtypes=(q_dtype,),
                                 rope=rope, rope_cols=qk_width, scale=DIFF_HEAD_DIM ** -0.5)
            if cache is None:
                mix = _diff_attn_prompt(proj, k_b, v_b, lam_params, p['subln_gain'][j],
                                        bsz=bsz, seq=seq, tq=256, lam_init=lam_init)
            else:
                cache_k, cache_v, page_table = cache
                mix = _decode_attn(proj[:, :qk_width].reshape(bsz, 1, qk_width),
                                   k_f32.reshape(bsz, 1, qk_width), v_f32.reshape(bsz, 1, qk_width),
                                   cache_k, cache_v, page_table, lam_params, p['subln_gain'][j],
                                   lam_init=lam_init).reshape(m, qk_width)
            q_cols = qk_width
        width = MEM_HEADS * MEM_HEAD_DIM
        mem_o = _mem_attn(proj.reshape(bsz, seq, proj.shape[1]), q_cols // width,
                          mem_k[l][0], mem_k[l][1], mem_v[l][0], mem_v[l][1], tq=tq_mem)
        x = _out_proj(mix, mem_o.reshape(m, width), p['w_out'][l], x, tm=tm)
        x = _ffn(x, p['norm_ffn2'][l], p['w_ffn2_up'][l], p['w_ffn2_down'][l], tm=tm, tf=tf,
                 final_gain=p['norm_final'] if l == depth - 1 else None)
    return x, new_conv, k_f32, v_f32


def kernel(x_prompt, x_sample, state_conv, cache_k, cache_v, cache_mem_k, cache_mem_v, page_table, mem_prompt, norm_ffn1, w_ffn1_up, w_ffn1_down, norm_mix, w_in_a, conv_w, w_in_b, lambda_q1, lambda_k1, lambda_q2, lambda_k2, subln_gain, norm_mem, w_mem_kv, w_out, norm_ffn2, w_ffn2_up, w_ffn2_down, norm_kv, w_kv, norm_final):
    bsz_p, s_p, d = x_prompt.shape
    bsz_s, s_s, _ = x_sample.shape
    depth = norm_ffn1.shape[0]
    n_mem = mem_prompt.shape[1]
    mem_width = MEM_HEADS * MEM_HEAD_DIM
    n_pool, page = cache_k.shape[0], cache_k.shape[1]
    k_heads, v_heads = cache_k.shape[2], cache_v.shape[2]
    qk_width = k_heads * cache_k.shape[3]
    past_len = page_table.shape[1] * page
    assert s_s == 1

    p = {
        'norm_ffn1': norm_ffn1, 'w_ffn1_up': w_ffn1_up.astype(BF16), 'w_ffn1_down': w_ffn1_down.astype(BF16),
        'norm_mix': norm_mix, 'w_in_a': w_in_a.astype(BF16), 'conv_w': conv_w, 'w_in_b': w_in_b.astype(BF16),
        'lambda_q1': lambda_q1, 'lambda_k1': lambda_k1, 'lambda_q2': lambda_q2, 'lambda_k2': lambda_k2,
        'subln_gain': subln_gain, 'w_out': w_out.astype(BF16),
        'norm_ffn2': norm_ffn2, 'w_ffn2_up': w_ffn2_up.astype(BF16), 'w_ffn2_down': w_ffn2_down.astype(BF16),
        'norm_kv': norm_kv, 'w_kv': w_kv.astype(BF16), 'norm_final': norm_final,
    }
    w_mem_kv_b = w_mem_kv.astype(BF16)

    mem_rows = mem_prompt.reshape(bsz_p * n_mem, d)
    mem_kv = [_norm_matmul(mem_rows, norm_mem[l], w_mem_kv_b[l], tm=bsz_p * n_mem, tn=512)[0]
              .reshape(bsz_p, n_mem, 2 * mem_width) for l in range(depth)]
    mem_k_prompt = jnp.stack([kv[..., :mem_width] for kv in mem_kv], axis=0)
    mem_v_prompt = jnp.stack([kv[..., mem_width:] for kv in mem_kv], axis=0)
    y_p, conv_p, k_p, v_p = _trunk(
        x_prompt.reshape(bsz_p * s_p, d), p, bsz=bsz_p, seq=s_p, tm=512,
        rope=_rope_tables(jnp.arange(s_p)),
        mem_k=[(kv, 0) for kv in mem_kv], mem_v=[(kv, 1) for kv in mem_kv],
        conv_state=None, cache=None, depth=depth)

    rows_s = bsz_s * s_s
    pos_s = jnp.full((rows_s,), past_len, jnp.int32)
    cmk = cache_mem_k.reshape(depth, bsz_s, n_mem, mem_width)
    cmv = cache_mem_v.reshape(depth, bsz_s, n_mem, mem_width)
    y_s, conv_s, k_s, v_s = _trunk(
        x_sample.reshape(rows_s, d), p, bsz=bsz_s, seq=s_s, tm=rows_s,
        rope=_rope_tables(pos_s),
        mem_k=[(cmk[l], 0) for l in range(depth)], mem_v=[(cmv[l], 0) for l in range(depth)],
        conv_state=[(state_conv[l, :, 0], state_conv[l, :, 1]) for l in range(depth // 2)],
        cache=(jnp.transpose(cache_k, (0, 2, 3, 1)).reshape(n_pool, qk_width, page),
               jnp.transpose(cache_v, (0, 2, 1, 3)), page_table),
        depth=depth)

    return (y_p.reshape(bsz_p, s_p, d),
            y_s.reshape(bsz_s, s_s, d),
            jnp.stack(conv_p, axis=0),
            jnp.stack(conv_s, axis=0),
            k_p.reshape(bsz_p, s_p, k_heads, -1),
            v_p.reshape(bsz_p, s_p, v_heads, -1),
            k_s.reshape(bsz_s, s_s, k_heads, -1),
            v_s.reshape(bsz_s, s_s, v_heads, -1),
            mem_k_prompt.reshape(depth, bsz_p, n_mem, MEM_HEADS, MEM_HEAD_DIM),
            mem_v_prompt.reshape(depth, bsz_p, n_mem, MEM_HEADS, MEM_HEAD_DIM))
```

```python
import functools
import math
from typing import NamedTuple

import jax
import jax.numpy as jnp
from jax import lax
from jax.experimental import pallas as pl
from jax.experimental.pallas import tpu as pltpu

F32 = jnp.float32
BF16 = jnp.bfloat16

NORM_EPS = 1e-6
ROPE_THETA = 10000.0
LANES = 128
VMEM_LIMIT = 56 * 1024 * 1024
PROJ_CHUNK = 512

MEM_HEADS = 4
MEM_HEAD_DIM = 128
DIFF_HEAD_DIM = 64
CONV_WIDTH = 3
DECODE_PAGES_PER_STEP = 8
DECODE_ROWS = 32


def _params(*sem):
    return pltpu.CompilerParams(dimension_semantics=sem, vmem_limit_bytes=VMEM_LIMIT)


def _rms(x, g):
    ms = jnp.mean(x * x, axis=-1, keepdims=True)
    return x * lax.rsqrt(ms + NORM_EPS) * g


def _rope_tile(y, cos, sin_lo, sin_hi):
    outs = []
    for c in range(y.shape[1] // LANES):
        yc = y[:, c * LANES:(c + 1) * LANES]
        up = pltpu.roll(yc, LANES - 32, axis=1)
        dn = pltpu.roll(yc, 32, axis=1)
        outs.append(yc * cos + up * sin_lo + dn * sin_hi)
    return jnp.concatenate(outs, axis=1) if len(outs) > 1 else outs[0]


class _Segment(NamedTuple):
    n_cols: int
    out_dtypes: tuple
    rope: bool = False
    scale: float = 1.0


def _norm_proj_kernel(*refs, segments, has_rope):
    if has_rope:
        x_ref, g_ref, w_ref, cos_ref, slo_ref, shi_ref = refs[:6]
        out_refs = refs[6:]
    else:
        x_ref, g_ref, w_ref = refs[:3]
        out_refs = refs[3:]
    h = _rms(x_ref[...], g_ref[...]).astype(BF16)
    col, k = 0, 0
    for seg in segments:
        outs = out_refs[k:k + len(seg.out_dtypes)]
        for c0 in range(0, seg.n_cols, PROJ_CHUNK):
            c1 = min(c0 + PROJ_CHUNK, seg.n_cols)
            y = jnp.dot(h, w_ref[:, col + c0:col + c1], preferred_element_type=F32)
            if seg.rope:
                y = _rope_tile(y, cos_ref[...], slo_ref[...], shi_ref[...]) * seg.scale
            for o in outs:
                o[:, c0:c1] = y.astype(o.dtype)
        col += seg.n_cols
        k += len(seg.out_dtypes)


def _norm_proj(x, g, w, segments, *, tm, rope=None):
    m, d = x.shape
    assert m % tm == 0 and sum(s.n_cols for s in segments) == w.shape[1]
    has_rope = any(s.rope for s in segments)
    in_specs = [
        pl.BlockSpec((tm, d), lambda i: (i, 0)),
        pl.BlockSpec((1, d), lambda i: (0, 0)),
        pl.BlockSpec(w.shape, lambda i: (0, 0), pipeline_mode=pl.Buffered(1)),
    ]
    args = [x, g.reshape(1, d), w]
    if has_rope:
        n_tab = rope[0].shape[0] // tm
        for t in rope:
            in_specs.append(pl.BlockSpec((tm, LANES), lambda i: (i % n_tab, 0)))
            args.append(t)
    out_specs, out_shape = [], []
    for seg in segments:
        for dt in seg.out_dtypes:
            out_specs.append(pl.BlockSpec((tm, seg.n_cols), lambda i: (i, 0)))
            out_shape.append(jax.ShapeDtypeStruct((m, seg.n_cols), dt))
    return pl.pallas_call(
        functools.partial(_norm_proj_kernel, segments=tuple(segments), has_rope=has_rope),
        grid=(m // tm,),
        in_specs=in_specs,
        out_specs=out_specs,
        out_shape=out_shape,
        compiler_params=_params("parallel"),
    )(*args)


def _ffn_kernel(*refs, final_norm):
    if final_norm:
        x_ref, g_ref, wg_ref, wu_ref, wd_ref, gf_ref, o_ref, h_ref = refs
    else:
        x_ref, g_ref, wg_ref, wu_ref, wd_ref, o_ref, h_ref = refs
    j = pl.program_id(1)

    @pl.when(j == 0)
    def _():
        x = x_ref[...]
        h_ref[...] = _rms(x, g_ref[...]).astype(BF16)
        o_ref[...] = x

    h = h_ref[...]
    gate = jnp.dot(h, wg_ref[...], preferred_element_type=F32)
    up = jnp.dot(h, wu_ref[...], preferred_element_type=F32)
    act = (gate * jax.nn.sigmoid(gate) * up * 0.5).astype(BF16)
    o_ref[...] += jnp.dot(act, wd_ref[...], preferred_element_type=F32)

    if final_norm:
        @pl.when(j == pl.num_programs(1) - 1)
        def _():
            o_ref[...] = _rms(o_ref[...], gf_ref[...])


def _ffn(x, g, w_up, w_down, *, tm, tf, final_gain=None):
    m, d = x.shape
    f = w_down.shape[0]
    assert m % tm == 0 and f % tf == 0 and w_up.shape[1] == 2 * f
    nf = f // tf
    in_specs = [
        pl.BlockSpec((tm, d), lambda i, j: (i, 0), pipeline_mode=pl.Buffered(1)),
        pl.BlockSpec((1, d), lambda i, j: (0, 0)),
        pl.BlockSpec((d, tf), lambda i, j: (0, j)),
        pl.BlockSpec((d, tf), lambda i, j: (0, nf + j)),
        pl.BlockSpec((tf, d), lambda i, j: (j, 0)),
    ]
    args = [x, g.reshape(1, d), w_up, w_up, w_down]
    if final_gain is not None:
        in_specs.append(pl.BlockSpec((1, d), lambda i, j: (0, 0)))
        args.append(final_gain.reshape(1, d))
    return pl.pallas_call(
        functools.partial(_ffn_kernel, final_norm=final_gain is not None),
        grid=(m // tm, nf),
        in_specs=in_specs,
        out_specs=pl.BlockSpec((tm, d), lambda i, j: (i, 0)),
        out_shape=jax.ShapeDtypeStruct((m, d), F32),
        scratch_shapes=[pltpu.VMEM((tm, d), BF16)],
        compiler_params=_params("parallel", "arbitrary"),
    )(*args)


def _conv_gate_kernel(b_ref, c_ref, xin_ref, w_ref, mix_ref, st_ref, ubuf, *, tiles_per_seq):
    tm = b_ref.shape[0]
    i = pl.program_id(0)

    @pl.when(i % tiles_per_seq == 0)
    def _():
        ubuf[0:8, :] = jnp.zeros((8, ubuf.shape[1]), F32)

    u = c_ref[...].astype(F32) * xin_ref[...].astype(F32)
    ubuf[8:tm + 8, :] = u
    w = w_ref[...]
    conv = w[0:1] * ubuf[6:tm + 6, :] + w[1:2] * ubuf[7:tm + 7, :] + w[2:3] * u
    mix_ref[...] = (b_ref[...].astype(F32) * conv).astype(mix_ref.dtype)
    st_ref[...] = u[tm - (CONV_WIDTH - 1):tm]
    ubuf[0:8, :] = u[tm - 8:tm]


def _conv_gate_prompt(proj, conv_w, *, bsz, tm, cdim):
    m = proj.shape[0]
    tiles_per_seq = m // bsz // tm
    col = lambda cb: pl.BlockSpec((tm, cdim), lambda i: (i, cb))
    return pl.pallas_call(
        functools.partial(_conv_gate_kernel, tiles_per_seq=tiles_per_seq),
        grid=(m // tm,),
        in_specs=[col(0), col(1), col(2), pl.BlockSpec((CONV_WIDTH, cdim), lambda i: (0, 0))],
        out_specs=[pl.BlockSpec((tm, cdim), lambda i: (i, 0)),
                   pl.BlockSpec((None, CONV_WIDTH - 1, cdim), lambda i: (i // tiles_per_seq, 0, 0))],
        out_shape=[jax.ShapeDtypeStruct((m, cdim), BF16),
                   jax.ShapeDtypeStruct((bsz, CONV_WIDTH - 1, cdim), F32)],
        scratch_shapes=[pltpu.VMEM((tm + 8, cdim), F32)],
        compiler_params=_params("arbitrary"),
    )(proj, proj, proj, conv_w)


def _conv_gate_step_kernel(b_ref, c_ref, xin_ref, s0_ref, s1_ref, w_ref, mix_ref, u_ref):
    u = c_ref[...].astype(F32) * xin_ref[...].astype(F32)
    w = w_ref[...]
    conv = w[0:1] * s0_ref[...] + w[1:2] * s1_ref[...] + w[2:3] * u
    mix_ref[...] = (b_ref[...].astype(F32) * conv).astype(mix_ref.dtype)
    u_ref[...] = u


def _conv_gate_step(proj, s0, s1, conv_w, *, cdim):
    m = proj.shape[0]
    col = lambda cb: pl.BlockSpec((m, cdim), lambda i: (0, cb))
    full = lambda r: pl.BlockSpec((r, cdim), lambda i: (0, 0))
    return pl.pallas_call(
        _conv_gate_step_kernel,
        grid=(1,),
        in_specs=[col(0), col(1), col(2), full(m), full(m), full(CONV_WIDTH)],
        out_specs=[full(m), full(m)],
        out_shape=[jax.ShapeDtypeStruct((m, cdim), BF16), jax.ShapeDtypeStruct((m, cdim), F32)],
        compiler_params=_params("arbitrary"),
    )(proj, proj, proj, s0, s1, conv_w)


def _mem_attn_kernel(q_ref, k_ref, v_ref, o_ref):
    rows = q_ref.shape[0]
    q = q_ref[...].astype(BF16)
    if rows < 8:
        q = jnp.broadcast_to(q[0:1], (8, q.shape[1]))
    k = k_ref[...].astype(BF16)
    v = v_ref[...].astype(BF16)
    scale = MEM_HEAD_DIM ** -0.5
    for h in range(MEM_HEADS):
        hs = slice(h * MEM_HEAD_DIM, (h + 1) * MEM_HEAD_DIM)
        s = lax.dot_general(q[:, hs], k[:, hs], (((1,), (1,)), ((), ())),
                            preferred_element_type=F32) * scale
        p = jnp.exp(s - jnp.max(s, axis=-1, keepdims=True))
        l = jnp.sum(p, axis=-1, keepdims=True)
        o = jnp.dot(p.astype(BF16), v[:, hs], preferred_element_type=F32) / l
        o_ref[:, hs] = o[:rows].astype(o_ref.dtype)


def _mem_attn(q3, k3, v3, *, tq):
    nb, rows, width = q3.shape
    n_mem = k3.shape[1]
    return pl.pallas_call(
        _mem_attn_kernel,
        grid=(nb, rows // tq),
        in_specs=[pl.BlockSpec((None, tq, width), lambda b, i: (b, i, 0)),
                  pl.BlockSpec((None, n_mem, width), lambda b, i: (b, 0, 0)),
                  pl.BlockSpec((None, n_mem, width), lambda b, i: (b, 0, 0))],
        out_specs=pl.BlockSpec((None, tq, width), lambda b, i: (b, i, 0)),
        out_shape=jax.ShapeDtypeStruct((nb, rows, width), BF16),
        compiler_params=_params("parallel", "parallel"),
    )(q3, k3, v3)


def _out_proj_kernel(a1_ref, a2_ref, w_ref, x_ref, o_ref):
    n1 = a1_ref.shape[1]
    acc = jnp.dot(a1_ref[...], w_ref[0:n1, :], preferred_element_type=F32)
    acc = acc + jnp.dot(a2_ref[...], w_ref[n1:, :], preferred_element_type=F32)
    o_ref[...] = x_ref[...] + acc


def _out_proj(a1, a2, w, x, *, tm):
    m, d = x.shape
    n1, n2 = a1.shape[1], a2.shape[1]
    return pl.pallas_call(
        _out_proj_kernel,
        grid=(m // tm,),
        in_specs=[pl.BlockSpec((tm, n1), lambda i: (i, 0)),
                  pl.BlockSpec((tm, n2), lambda i: (i, 0)),
                  pl.BlockSpec((n1 + n2, d), lambda i: (0, 0), pipeline_mode=pl.Buffered(1)),
                  pl.BlockSpec((tm, d), lambda i: (i, 0))],
        out_specs=pl.BlockSpec((tm, d), lambda i: (i, 0)),
        out_shape=jax.ShapeDtypeStruct((m, d), F32),
        compiler_params=_params("parallel"),
    )(a1, a2, w, x)


def _lambda_value(lq1_ref, lk1_ref, lq2_ref, lk2_ref, lam_init):
    a = jnp.exp(jnp.sum(lq1_ref[...] * lk1_ref[...], axis=-1, keepdims=True))
    b = jnp.exp(jnp.sum(lq2_ref[...] * lk2_ref[...], axis=-1, keepdims=True))
    return a - b + lam_init


def _subln(d, gain, lam_init):
    return _rms(d, gain) * (1.0 - lam_init)


def _diff_attn_kernel(q_ref, k_ref, v_ref, lq1_ref, lk1_ref, lq2_ref, lk2_ref, g_ref, o_ref,
                      q2_ref, vt_ref, m_ref, l_ref, acc_ref, *, lam_init):
    tq = q_ref.shape[0]
    tk = vt_ref.shape[2]
    qi = pl.program_id(2)

    @pl.when(qi == 0)
    def _():
        for c in range(vt_ref.shape[0]):
            vt_ref[c] = v_ref[c * tk:(c + 1) * tk, :].T

    q = q_ref[...]
    lane = lax.broadcasted_iota(jnp.int32, q.shape, 1)
    zero = jnp.zeros_like(q)
    q2_ref[0:tq, :] = jnp.where(lane < DIFF_HEAD_DIM, q, zero)
    q2_ref[tq:2 * tq, :] = jnp.where(lane < DIFF_HEAD_DIM, zero, q)
    m_ref[...] = jnp.full(m_ref.shape, -jnp.inf, F32)
    l_ref[...] = jnp.zeros(l_ref.shape, F32)
    acc_ref[...] = jnp.zeros(acc_ref.shape, F32)

    def block(kj, masked):
        start = pl.multiple_of(kj * tk, tk)
        k = k_ref[pl.ds(start, tk), :]
        st = lax.dot_general(k, q2_ref[...], (((1,), (1,)), ((), ())), preferred_element_type=F32)
        if masked:
            key = lax.broadcasted_iota(jnp.int32, st.shape, 0)
            qry = lax.broadcasted_iota(jnp.int32, st.shape, 1)
            qry = jnp.where(qry >= tq, qry - tq, qry)
            st = jnp.where(key <= qry, st, -jnp.inf)
        m_prev = m_ref[...]
        m_next = jnp.maximum(m_prev, jnp.max(st, axis=0, keepdims=True))
        alpha = jnp.exp(m_prev - m_next)
        p = jnp.exp(st - m_next)
        l_ref[...] = alpha * l_ref[...] + jnp.sum(p, axis=0, keepdims=True)
        acc_ref[...] = alpha * acc_ref[...] + jnp.dot(vt_ref[kj], p.astype(BF16),
                                                      preferred_element_type=F32)
        m_ref[...] = m_next

    def body(kj, carry):
        block(kj, False)
        return carry

    lax.fori_loop(0, qi, body, 0)
    block(qi, True)

    ot = acc_ref[...] / l_ref[...]
    lam = _lambda_value(lq1_ref, lk1_ref, lq2_ref, lk2_ref, lam_init)
    dt = ot[:, 0:tq] - lam * ot[:, tq:2 * tq]
    yt = dt * lax.rsqrt(jnp.mean(dt * dt, axis=0, keepdims=True) + NORM_EPS)
    o_ref[...] = (yt.T * g_ref[...] * (1.0 - lam_init)).astype(o_ref.dtype)


def _diff_attn_prompt(q, k, v, lam_params, gain, *, bsz, seq, tq, lam_init):
    m = bsz * seq
    width = k.shape[1]
    heads = width // LANES
    nq = seq // tq
    small = lambda a: pl.BlockSpec(a.shape, lambda b, h, i: (0, 0))
    lam_args = [a.reshape(1, -1) for a in lam_params]
    gain = gain.reshape(1, -1)
    return pl.pallas_call(
        functools.partial(_diff_attn_kernel, lam_init=lam_init),
        grid=(bsz, heads, nq),
        in_specs=[pl.BlockSpec((tq, LANES), lambda b, h, i: (b * nq + i, h)),
                  pl.BlockSpec((seq, LANES), lambda b, h, i: (b, h)),
                  pl.BlockSpec((seq, LANES), lambda b, h, i: (b, h))]
                 + [small(a) for a in lam_args] + [small(gain)],
        out_specs=pl.BlockSpec((tq, LANES), lambda b, h, i: (b * nq + i, h)),
        out_shape=jax.ShapeDtypeStruct((m, width), BF16),
        scratch_shapes=[pltpu.VMEM((2 * tq, LANES), BF16),
                        pltpu.VMEM((seq // tq, LANES, tq), BF16),
                        pltpu.VMEM((1, 2 * tq), F32),
                        pltpu.VMEM((1, 2 * tq), F32),
                        pltpu.VMEM((LANES, 2 * tq), F32)],
        compiler_params=_params("parallel", "parallel", "arbitrary"),
    )(q, k, v, *lam_args, gain)


def _decode_attn_kernel(pt_ref, q_ref, kn_ref, vn_ref, *refs, n_pages, lam_init):
    del pt_ref
    k_refs, v_refs = refs[:n_pages], refs[n_pages:2 * n_pages]
    lq1_ref, lk1_ref, lq2_ref, lk2_ref, g_ref, o_ref, qbd_ref, m_ref, l_ref, acc_ref = refs[2 * n_pages:]
    g = pl.program_id(1)
    width = q_ref.shape[1]

    @pl.when(g == 0)
    def _():
        row = lax.broadcasted_iota(jnp.int32, (DECODE_ROWS, width), 0)
        lane = lax.broadcasted_iota(jnp.int32, (DECODE_ROWS, width), 1)
        qb = jnp.broadcast_to(q_ref[...], (DECODE_ROWS, width))
        keep = (lane >= row * DIFF_HEAD_DIM) & (lane < (row + 1) * DIFF_HEAD_DIM)
        qbd_ref[...] = jnp.where(keep, qb, 0.0).astype(BF16)
        m_ref[...] = jnp.full(m_ref.shape, -jnp.inf, F32)
        l_ref[...] = jnp.zeros(l_ref.shape, F32)
        acc_ref[...] = jnp.zeros(acc_ref.shape, F32)

    qbd = qbd_ref[...]
    s = jnp.concatenate(
        [jnp.dot(qbd, kr[...].astype(BF16), preferred_element_type=F32) for kr in k_refs], axis=1)
    m_prev = m_ref[:, 0:1]
    m_next = jnp.maximum(m_prev, jnp.max(s, axis=-1, keepdims=True))
    alpha = jnp.exp(m_prev - m_next)
    p = jnp.exp(s - m_next)
    l_next = alpha * l_ref[:, 0:1] + jnp.sum(p, axis=-1, keepdims=True)
    pb = p.astype(BF16)
    page = k_refs[0].shape[1]
    for h in range(width // LANES):
        hs = slice(h * LANES, (h + 1) * LANES)
        pv = jnp.dot(pb[:, 0:page], v_refs[0][h].astype(BF16), preferred_element_type=F32)
        for r in range(1, n_pages):
            pv = pv + jnp.dot(pb[:, r * page:(r + 1) * page], v_refs[r][h].astype(BF16),
                              preferred_element_type=F32)
        acc_ref[:, hs] = alpha * acc_ref[:, hs] + pv
    m_ref[...] = jnp.broadcast_to(m_next, m_ref.shape)
    l_ref[...] = jnp.broadcast_to(l_next, l_ref.shape)

    @pl.when(g == pl.num_programs(1) - 1)
    def _():
        kn = kn_ref[...].astype(BF16).astype(F32)
        s_new = jnp.sum(qbd_ref[...].astype(F32) * kn, axis=-1, keepdims=True)
        m_old = m_ref[:, 0:1]
        m_fin = jnp.maximum(m_old, s_new)
        a = jnp.exp(m_old - m_fin)
        p_new = jnp.exp(s_new - m_fin)
        l_fin = a * l_ref[:, 0:1] + p_new
        o = (a * acc_ref[...] + p_new * vn_ref[...]) / l_fin
        lam = _lambda_value(lq1_ref, lk1_ref, lq2_ref, lk2_ref, lam_init)
        gain = g_ref[...]
        for h in range(width // LANES):
            hs = slice(h * LANES, (h + 1) * LANES)
            d = o[2 * h:2 * h + 1, hs] - lam * o[2 * h + 1:2 * h + 2, hs]
            o_ref[:, hs] = _subln(d, gain, lam_init).astype(o_ref.dtype)


def _decode_attn(q, k_new, v_new, cache_k, cache_v, page_table, lam_params, gain, *, lam_init):
    nb, _, width = q.shape
    page = cache_k.shape[2]
    n_used = page_table.shape[1]
    npg = DECODE_PAGES_PER_STEP
    assert n_used % npg == 0
    row = pl.BlockSpec((None, 1, width), lambda b, g, pt: (b, 0, 0))
    small = lambda a: pl.BlockSpec(a.shape, lambda b, g, pt: (0, 0))
    k_spec = lambda r: pl.BlockSpec((None, width, page),
                                    lambda b, g, pt: (pt[b, g * npg + r], 0, 0))
    v_spec = lambda r: pl.BlockSpec((None, width // LANES, page, LANES),
                                    lambda b, g, pt: (pt[b, g * npg + r], 0, 0, 0))
    lam_args = [a.reshape(1, -1) for a in lam_params]
    gain = gain.reshape(1, -1)
    grid_spec = pltpu.PrefetchScalarGridSpec(
        num_scalar_prefetch=1,
        grid=(nb, n_used // npg),
        in_specs=[row, row, row] + [k_spec(r) for r in range(npg)] + [v_spec(r) for r in range(npg)]
                 + [small(a) for a in lam_args] + [small(gain)],
        out_specs=pl.BlockSpec((None, 1, width), lambda b, g, pt: (b, 0, 0)),
        scratch_shapes=[pltpu.VMEM((DECODE_ROWS, width), BF16),
                        pltpu.VMEM((DECODE_ROWS, LANES), F32),
                        pltpu.VMEM((DECODE_ROWS, LANES), F32),
                        pltpu.VMEM((DECODE_ROWS, width), F32)],
    )
    return pl.pallas_call(
        functools.partial(_decode_attn_kernel, n_pages=npg, lam_init=lam_init),
        grid_spec=grid_spec,
        out_shape=jax.ShapeDtypeStruct((nb, 1, width), BF16),
        compiler_params=_params("parallel", "arbitrary"),
    )(page_table, q, k_new, v_new, *([cache_k] * npg), *([cache_v] * npg), *lam_args, gain)


def _rope_tables(pos):
    half = DIFF_HEAD_DIM // 2
    inv_freq = ROPE_THETA ** (-jnp.arange(half, dtype=F32) / half)
    ang = pos.astype(F32)[:, None] * inv_freq[None, :]
    cos, sin, zero = jnp.cos(ang), jnp.sin(ang), jnp.zeros_like(ang)
    reps = LANES // DIFF_HEAD_DIM
    cos_t = jnp.tile(jnp.concatenate([cos, cos], axis=1), (1, reps))
    sin_lo = jnp.tile(jnp.concatenate([-sin, zero], axis=1), (1, reps))
    sin_hi = jnp.tile(jnp.concatenate([zero, sin], axis=1), (1, reps))
    return cos_t, sin_lo, sin_hi


def _trunk(x, p, *, bsz, seq, tm, tm_ffn, rope, mem_k, mem_v, conv_state, cache, depth):
    m, d = x.shape
    n_a = depth // 2
    tf = 512
    cdim = p['conv_w'].shape[-1]
    qk_width = p['w_kv'].shape[1] // 2
    mem_width = MEM_HEADS * MEM_HEAD_DIM
    tq_mem = min(tm, seq)
    act_dtype = BF16 if cache is None else F32
    new_conv = []
    k_f32 = v_f32 = k_b = v_b = None
    for l in range(depth):
        if l == n_a:
            k_f32, k_b, v_f32, v_b = _norm_proj(
                x, p['norm_kv'], p['w_kv'],
                [_Segment(qk_width, (F32, BF16), rope=True), _Segment(qk_width, (F32, BF16))],
                tm=tm, rope=rope)
        x = _ffn(x, p['norm_ffn1'][l], p['w_ffn1_up'][l], p['w_ffn1_down'][l], tm=tm_ffn, tf=tf)
        if l < n_a:
            proj, q_mem = _norm_proj(x, p['norm_mix'][l], p['w_in_a'][l],
                                     [_Segment(3 * cdim, (act_dtype,)), _Segment(mem_width, (act_dtype,))],
                                     tm=tm)
            if conv_state is None:
                mix, st = _conv_gate_prompt(proj, p['conv_w'][l], bsz=bsz, tm=tm, cdim=cdim)
            else:
                s0, s1 = conv_state[l]
                mix, u = _conv_gate_step(proj, s0, s1, p['conv_w'][l], cdim=cdim)
                st = jnp.stack([s1, u], axis=1)
            new_conv.append(st)
        else:
            j = l - n_a
            lam_init = 0.8 - 0.6 * math.exp(-0.3 * l)
            lam_params = (p['lambda_q1'][j], p['lambda_k1'][j], p['lambda_q2'][j], p['lambda_k2'][j])
            q, q_mem = _norm_proj(
                x, p['norm_mix'][l], p['w_in_b'][j],
                [_Segment(qk_width, (act_dtype,), rope=True, scale=DIFF_HEAD_DIM ** -0.5),
                 _Segment(mem_width, (act_dtype,))],
                tm=tm, rope=rope)
            if cache is None:
                mix = _diff_attn_prompt(q, k_b, v_b, lam_params, p['subln_gain'][j],
                                        bsz=bsz, seq=seq, tq=512, lam_init=lam_init)
            else:
                cache_k, cache_v, page_table = cache
                mix = _decode_attn(q.reshape(bsz, 1, qk_width),
                                   k_f32.reshape(bsz, 1, qk_width), v_f32.reshape(bsz, 1, qk_width),
                                   cache_k, cache_v, page_table, lam_params, p['subln_gain'][j],
                                   lam_init=lam_init).reshape(m, qk_width)
        mem_o = _mem_attn(q_mem.reshape(bsz, seq, mem_width), mem_k[l], mem_v[l], tq=tq_mem)
        x = _out_proj(mix, mem_o.reshape(m, mem_width), p['w_out'][l], x, tm=tm)
        x = _ffn(x, p['norm_ffn2'][l], p['w_ffn2_up'][l], p['w_ffn2_down'][l], tm=tm_ffn, tf=tf,
                 final_gain=p['norm_final'] if l == depth - 1 else None)
    return x, new_conv, k_f32, v_f32


def kernel(x_prompt, x_sample, state_conv, cache_k, cache_v, cache_mem_k, cache_mem_v, page_table, mem_prompt, norm_ffn1, w_ffn1_up, w_ffn1_down, norm_mix, w_in_a, conv_w, w_in_b, lambda_q1, lambda_k1, lambda_q2, lambda_k2, subln_gain, norm_mem, w_mem_kv, w_out, norm_ffn2, w_ffn2_up, w_ffn2_down, norm_kv, w_kv, norm_final):
    bsz_p, s_p, d = x_prompt.shape
    bsz_s, s_s, _ = x_sample.shape
    depth = norm_ffn1.shape[0]
    n_mem = mem_prompt.shape[1]
    mem_width = MEM_HEADS * MEM_HEAD_DIM
    n_pool, page = cache_k.shape[0], cache_k.shape[1]
    k_heads, v_heads = cache_k.shape[2], cache_v.shape[2]
    qk_width = k_heads * cache_k.shape[3]
    past_len = page_table.shape[1] * page
    assert s_s == 1

    def per_layer(w):
        return [w[l].astype(BF16) for l in range(w.shape[0])]

    p = {
        'norm_ffn1': norm_ffn1, 'w_ffn1_up': per_layer(w_ffn1_up), 'w_ffn1_down': per_layer(w_ffn1_down),
        'norm_mix': norm_mix, 'w_in_a': per_layer(w_in_a), 'conv_w': conv_w, 'w_in_b': per_layer(w_in_b),
        'lambda_q1': lambda_q1, 'lambda_k1': lambda_k1, 'lambda_q2': lambda_q2, 'lambda_k2': lambda_k2,
        'subln_gain': subln_gain, 'w_out': per_layer(w_out),
        'norm_ffn2': norm_ffn2, 'w_ffn2_up': per_layer(w_ffn2_up), 'w_ffn2_down': per_layer(w_ffn2_down),
        'norm_kv': norm_kv, 'w_kv': w_kv.astype(BF16), 'norm_final': norm_final,
    }
    w_mem_kv_b = per_layer(w_mem_kv)

    mem_rows = mem_prompt.reshape(bsz_p * n_mem, d)
    mem_kv = [_norm_proj(mem_rows, norm_mem[l], w_mem_kv_b[l],
                         [_Segment(mem_width, (F32,)), _Segment(mem_width, (F32,))], tm=n_mem)
              for l in range(depth)]
    mem_k_p = [kv[0].reshape(bsz_p, n_mem, mem_width) for kv in mem_kv]
    mem_v_p = [kv[1].reshape(bsz_p, n_mem, mem_width) for kv in mem_kv]
    y_p, conv_p, k_p, v_p = _trunk(
        x_prompt.reshape(bsz_p * s_p, d), p, bsz=bsz_p, seq=s_p, tm=512, tm_ffn=1024,
        rope=_rope_tables(jnp.arange(s_p)), mem_k=mem_k_p, mem_v=mem_v_p,
        conv_state=None, cache=None, depth=depth)

    rows_s = bsz_s * s_s
    pos_s = jnp.full((rows_s,), past_len, jnp.int32)
    cmk = cache_mem_k.reshape(depth, bsz_s, n_mem, mem_width)
    cmv = cache_mem_v.reshape(depth, bsz_s, n_mem, mem_width)
    y_s, conv_s, k_s, v_s = _trunk(
        x_sample.reshape(rows_s, d), p, bsz=bsz_s, seq=s_s, tm=rows_s, tm_ffn=rows_s,
        rope=_rope_tables(pos_s),
        mem_k=[cmk[l] for l in range(depth)], mem_v=[cmv[l] for l in range(depth)],
        conv_state=[(state_conv[l, :, 0], state_conv[l, :, 1]) for l in range(depth // 2)],
        cache=(jnp.transpose(cache_k, (0, 2, 3, 1)).reshape(n_pool, qk_width, page),
               jnp.transpose(cache_v, (0, 2, 1, 3)), page_table),
        depth=depth)

    mem_shape = (depth, bsz_p, n_mem, MEM_HEADS, MEM_HEAD_DIM)
    return (y_p.reshape(bsz_p, s_p, d),
            y_s.reshape(bsz_s, s_s, d),
            jnp.stack(conv_p, axis=0),
            jnp.stack(conv_s, axis=0),
            k_p.reshape(bsz_p, s_p, k_heads, -1),
            v_p.reshape(bsz_p, s_p, v_heads, -1),
            k_s.reshape(bsz_s, s_s, k_heads, -1),
            v_s.reshape(bsz_s, s_s, v_heads, -1),
            jnp.stack(mem_k_p, axis=0).reshape(mem_shape),
            jnp.stack(mem_v_p, axis=0).reshape(mem_shape))
```

```python
import functools
import math
from typing import NamedTuple

import jax
import jax.numpy as jnp
from jax import lax
from jax.experimental import pallas as pl
from jax.experimental.pallas import tpu as pltpu

F32 = jnp.float32
BF16 = jnp.bfloat16

NORM_EPS = 1e-6
ROPE_THETA = 10000.0
LOG2E = math.log2(math.e)
LANES = 128
VMEM_LIMIT = 56 * 1024 * 1024
PROJ_CHUNK = 512
CAST_BLOCK_BYTES = 8 * 1024 * 1024

MEM_HEADS = 4
MEM_HEAD_DIM = 128
DIFF_HEAD_DIM = 64
CONV_WIDTH = 3
ATTN_CHAIN_LANES = 256
DECODE_PAGES_PER_STEP = 8
DECODE_ROWS = 32


class _Tiles(NamedTuple):
    proj: int
    ffn: int
    ffn_cols: int
    attn: int


PROMPT_TILES = _Tiles(proj=512, ffn=1024, ffn_cols=512, attn=512)


def _params(*sem):
    return pltpu.CompilerParams(dimension_semantics=sem, vmem_limit_bytes=VMEM_LIMIT)


def _rms(x, g):
    ms = jnp.mean(x * x, axis=-1, keepdims=True)
    return x * lax.rsqrt(ms + NORM_EPS) * g


def _cast_kernel(w_ref, o_ref):
    o_ref[...] = w_ref[...].astype(o_ref.dtype)


def _cast_layer(w3, layer):
    _, k, n = w3.shape
    rows = k
    while rows * n * 4 > CAST_BLOCK_BYTES and rows % 16 == 0:
        rows //= 2
    return pl.pallas_call(
        _cast_kernel,
        grid=(k // rows,),
        in_specs=[pl.BlockSpec((None, rows, n), lambda i: (layer, i, 0))],
        out_specs=pl.BlockSpec((rows, n), lambda i: (i, 0)),
        out_shape=jax.ShapeDtypeStruct((k, n), BF16),
        compiler_params=_params("parallel"),
    )(w3)


def _rope_tile(y, cos, sin_lo, sin_hi):
    outs = []
    for c in range(y.shape[1] // LANES):
        yc = y[:, c * LANES:(c + 1) * LANES]
        up = pltpu.roll(yc, LANES - 32, axis=1)
        dn = pltpu.roll(yc, 32, axis=1)
        outs.append(yc * cos + up * sin_lo + dn * sin_hi)
    return jnp.concatenate(outs, axis=1) if len(outs) > 1 else outs[0]


class _Out(NamedTuple):
    dtype: object
    layout: str = 'rows'


class _Segment(NamedTuple):
    n_cols: int
    outs: tuple
    rope: bool = False
    scale: float = 1.0


def _norm_proj_kernel(*refs, segments, has_rope):
    if has_rope:
        x_ref, g_ref, w_ref, cos_ref, slo_ref, shi_ref = refs[:6]
        out_refs = refs[6:]
    else:
        x_ref, g_ref, w_ref = refs[:3]
        out_refs = refs[3:]
    h = _rms(x_ref[...], g_ref[...]).astype(BF16)
    col, k = 0, 0
    for seg in segments:
        outs = out_refs[k:k + len(seg.outs)]
        for c0 in range(0, seg.n_cols, PROJ_CHUNK):
            c1 = min(c0 + PROJ_CHUNK, seg.n_cols)
            y = jnp.dot(h, w_ref[:, col + c0:col + c1], preferred_element_type=F32)
            if seg.rope:
                y = _rope_tile(y, cos_ref[...], slo_ref[...], shi_ref[...]) * seg.scale
            for o, spec in zip(outs, seg.outs):
                if spec.layout == 'rows':
                    o[:, c0:c1] = y.astype(o.dtype)
                elif spec.layout == 'cols':
                    o[c0:c1, :] = y.T.astype(o.dtype)
                else:
                    for hh in range(c0 // LANES, c1 // LANES):
                        o[hh] = y[:, hh * LANES - c0:(hh + 1) * LANES - c0].astype(o.dtype)
        col += seg.n_cols
        k += len(seg.outs)


def _norm_proj(x, g, w, segments, *, tm, seq, rope=None):
    m, d = x.shape
    assert m % tm == 0 and sum(s.n_cols for s in segments) == w.shape[1]
    has_rope = any(s.rope for s in segments)
    row_major = all(o.layout == 'rows' for s in segments for o in s.outs)
    assert row_major or seq % tm == 0
    per_seq = max(seq // tm, 1)
    in_specs = [
        pl.BlockSpec((tm, d), lambda i: (i, 0)),
        pl.BlockSpec((1, d), lambda i: (0, 0)),
        pl.BlockSpec(w.shape, lambda i: (0, 0), pipeline_mode=pl.Buffered(1)),
    ]
    args = [x, g.reshape(1, d), w]
    if has_rope:
        n_tab = rope[0].shape[0] // tm
        for t in rope:
            in_specs.append(pl.BlockSpec((tm, LANES), lambda i: (i % n_tab, 0)))
            args.append(t)
    out_specs, out_shape = [], []
    for seg in segments:
        n = seg.n_cols
        for o in seg.outs:
            if o.layout == 'rows':
                out_specs.append(pl.BlockSpec((tm, n), lambda i: (i, 0)))
                out_shape.append(jax.ShapeDtypeStruct((m, n), o.dtype))
            elif o.layout == 'cols':
                out_specs.append(pl.BlockSpec((None, n, tm), lambda i: (i // per_seq, 0, i % per_seq)))
                out_shape.append(jax.ShapeDtypeStruct((m // seq, n, seq), o.dtype))
            else:
                out_specs.append(pl.BlockSpec((None, n // LANES, tm, LANES),
                                              lambda i: (i // per_seq, 0, i % per_seq, 0)))
                out_shape.append(jax.ShapeDtypeStruct((m // seq, n // LANES, seq, LANES), o.dtype))
    return pl.pallas_call(
        functools.partial(_norm_proj_kernel, segments=tuple(segments), has_rope=has_rope),
        grid=(m // tm,),
        in_specs=in_specs,
        out_specs=out_specs,
        out_shape=out_shape,
        compiler_params=_params("parallel"),
    )(*args)


def _ffn_kernel(*refs, final_norm, emit_weights):
    x_ref, g_ref, wg_ref, wu_ref, wd_ref = refs[:5]
    refs = refs[5:]
    if final_norm:
        gf_ref, refs = refs[0], refs[1:]
    o_ref, refs = refs[0], refs[1:]
    if emit_weights:
        wg_out, wu_out, wd_out = refs[:3]
        refs = refs[3:]
    h_ref, = refs
    j = pl.program_id(1)

    @pl.when(j == 0)
    def _():
        x = x_ref[...]
        h_ref[...] = _rms(x, g_ref[...]).astype(BF16)
        o_ref[...] = x

    wg = wg_ref[...].astype(BF16)
    wu = wu_ref[...].astype(BF16)
    wd = wd_ref[...].astype(BF16)
    if emit_weights:
        wg_out[...] = wg
        wu_out[...] = wu
        wd_out[...] = wd
    h = h_ref[...]
    gate = jnp.dot(h, wg, preferred_element_type=F32)
    up = jnp.dot(h, wu, preferred_element_type=F32)
    act = (gate * jax.nn.sigmoid(gate) * up * 0.5).astype(BF16)
    o_ref[...] += jnp.dot(act, wd, preferred_element_type=F32)

    if final_norm:
        @pl.when(j == pl.num_programs(1) - 1)
        def _():
            o_ref[...] = _rms(o_ref[...], gf_ref[...])


def _ffn(x, g, weights, *, tm, tf, final_gain=None):
    m, d = x.shape
    emit = isinstance(weights[2], int)
    if emit:
        w_up3, w_down3, layer = weights
        f = w_down3.shape[1]
        nf = f // tf
        w_specs = [pl.BlockSpec((None, d, tf), lambda i, j: (layer, 0, j)),
                   pl.BlockSpec((None, d, tf), lambda i, j: (layer, 0, nf + j)),
                   pl.BlockSpec((None, tf, d), lambda i, j: (layer, j, 0))]
        w_args = [w_up3, w_up3, w_down3]
    else:
        f = weights[2].shape[0]
        nf = f // tf
        w_specs = [pl.BlockSpec((d, tf), lambda i, j: (0, j)),
                   pl.BlockSpec((d, tf), lambda i, j: (0, j)),
                   pl.BlockSpec((tf, d), lambda i, j: (j, 0))]
        w_args = list(weights)
    assert m % tm == 0 and f % tf == 0
    in_specs = [
        pl.BlockSpec((tm, d), lambda i, j: (i, 0), pipeline_mode=pl.Buffered(1)),
        pl.BlockSpec((1, d), lambda i, j: (0, 0)),
    ] + w_specs
    args = [x, g.reshape(1, d)] + w_args
    if final_gain is not None:
        in_specs.append(pl.BlockSpec((1, d), lambda i, j: (0, 0)))
        args.append(final_gain.reshape(1, d))
    out_specs = [pl.BlockSpec((tm, d), lambda i, j: (i, 0))]
    out_shape = [jax.ShapeDtypeStruct((m, d), F32)]
    if emit:
        assert m == tm
        out_specs += [pl.BlockSpec((d, tf), lambda i, j: (0, j)),
                      pl.BlockSpec((d, tf), lambda i, j: (0, j)),
                      pl.BlockSpec((tf, d), lambda i, j: (j, 0))]
        out_shape += [jax.ShapeDtypeStruct((d, f), BF16), jax.ShapeDtypeStruct((d, f), BF16),
                      jax.ShapeDtypeStruct((f, d), BF16)]
    outs = pl.pallas_call(
        functools.partial(_ffn_kernel, final_norm=final_gain is not None, emit_weights=emit),
        grid=(m // tm, nf),
        in_specs=in_specs,
        out_specs=out_specs,
        out_shape=out_shape,
        scratch_shapes=[pltpu.VMEM((tm, d), BF16)],
        compiler_params=_params("parallel", "arbitrary"),
    )(*args)
    return (outs[0], tuple(outs[1:])) if emit else outs[0]


def _conv_gate_kernel(b_ref, c_ref, xin_ref, w_ref, mix_ref, st_ref, ubuf, *, tiles_per_seq):
    tm = b_ref.shape[0]
    i = pl.program_id(0)

    @pl.when(i % tiles_per_seq == 0)
    def _():
        ubuf[0:8, :] = jnp.zeros((8, ubuf.shape[1]), F32)

    u = c_ref[...].astype(F32) * xin_ref[...].astype(F32)
    ubuf[8:tm + 8, :] = u
    w = w_ref[...]
    conv = w[0:1] * ubuf[6:tm + 6, :] + w[1:2] * ubuf[7:tm + 7, :] + w[2:3] * u
    mix_ref[...] = (b_ref[...].astype(F32) * conv).astype(mix_ref.dtype)
    st_ref[...] = u[tm - (CONV_WIDTH - 1):tm]
    ubuf[0:8, :] = u[tm - 8:tm]


def _conv_gate_prompt(proj, conv_w, *, bsz, tm, cdim):
    m = proj.shape[0]
    tiles_per_seq = m // bsz // tm
    col = lambda cb: pl.BlockSpec((tm, cdim), lambda i: (i, cb))
    return pl.pallas_call(
        functools.partial(_conv_gate_kernel, tiles_per_seq=tiles_per_seq),
        grid=(m // tm,),
        in_specs=[col(0), col(1), col(2), pl.BlockSpec((CONV_WIDTH, cdim), lambda i: (0, 0))],
        out_specs=[pl.BlockSpec((tm, cdim), lambda i: (i, 0)),
                   pl.BlockSpec((None, CONV_WIDTH - 1, cdim), lambda i: (i // tiles_per_seq, 0, 0))],
        out_shape=[jax.ShapeDtypeStruct((m, cdim), BF16),
                   jax.ShapeDtypeStruct((bsz, CONV_WIDTH - 1, cdim), F32)],
        scratch_shapes=[pltpu.VMEM((tm + 8, cdim), F32)],
        compiler_params=_params("arbitrary"),
    )(proj, proj, proj, conv_w)


def _conv_gate_step_kernel(b_ref, c_ref, xin_ref, s0_ref, s1_ref, w_ref, mix_ref, u_ref):
    u = c_ref[...].astype(F32) * xin_ref[...].astype(F32)
    w = w_ref[...]
    conv = w[0:1] * s0_ref[...] + w[1:2] * s1_ref[...] + w[2:3] * u
    mix_ref[...] = (b_ref[...].astype(F32) * conv).astype(mix_ref.dtype)
    u_ref[...] = u


def _conv_gate_step(proj, s0, s1, conv_w, *, cdim):
    m = proj.shape[0]
    col = lambda cb: pl.BlockSpec((m, cdim), lambda i: (0, cb))
    full = lambda r: pl.BlockSpec((r, cdim), lambda i: (0, 0))
    return pl.pallas_call(
        _conv_gate_step_kernel,
        grid=(1,),
        in_specs=[col(0), col(1), col(2), full(m), full(m), full(CONV_WIDTH)],
        out_specs=[full(m), full(m)],
        out_shape=[jax.ShapeDtypeStruct((m, cdim), BF16), jax.ShapeDtypeStruct((m, cdim), F32)],
        compiler_params=_params("arbitrary"),
    )(proj, proj, proj, s0, s1, conv_w)


def _mem_attn_kernel(q_ref, k_ref, v_ref, o_ref):
    rows = q_ref.shape[0]
    q = q_ref[...].astype(BF16)
    if rows < 8:
        q = jnp.broadcast_to(q[0:1], (8, q.shape[1]))
    k = k_ref[...].astype(BF16)
    v = v_ref[...].astype(BF16)
    scale = MEM_HEAD_DIM ** -0.5
    for h in range(MEM_HEADS):
        hs = slice(h * MEM_HEAD_DIM, (h + 1) * MEM_HEAD_DIM)
        s = lax.dot_general(q[:, hs], k[:, hs], (((1,), (1,)), ((), ())),
                            preferred_element_type=F32) * scale
        p = jnp.exp(s - jnp.max(s, axis=-1, keepdims=True))
        l = jnp.sum(p, axis=-1, keepdims=True)
        o = jnp.dot(p.astype(BF16), v[:, hs], preferred_element_type=F32) / l
        o_ref[:, hs] = o[:rows].astype(o_ref.dtype)


def _mem_attn(q3, k3, v3, *, tq):
    nb, rows, width = q3.shape
    n_mem = k3.shape[1]
    return pl.pallas_call(
        _mem_attn_kernel,
        grid=(nb, rows // tq),
        in_specs=[pl.BlockSpec((None, tq, width), lambda b, i: (b, i, 0)),
                  pl.BlockSpec((None, n_mem, width), lambda b, i: (b, 0, 0)),
                  pl.BlockSpec((None, n_mem, width), lambda b, i: (b, 0, 0))],
        out_specs=pl.BlockSpec((None, tq, width), lambda b, i: (b, i, 0)),
        out_shape=jax.ShapeDtypeStruct((nb, rows, width), BF16),
        compiler_params=_params("parallel", "parallel"),
    )(q3, k3, v3)


def _out_proj_kernel(a1_ref, a2_ref, w_ref, x_ref, o_ref):
    n1 = a1_ref.shape[1]
    acc = jnp.dot(a1_ref[...], w_ref[0:n1, :], preferred_element_type=F32)
    acc = acc + jnp.dot(a2_ref[...], w_ref[n1:, :], preferred_element_type=F32)
    o_ref[...] = x_ref[...] + acc


def _out_proj(a1, a2, w, x, *, tm):
    m, d = x.shape
    n1, n2 = a1.shape[1], a2.shape[1]
    return pl.pallas_call(
        _out_proj_kernel,
        grid=(m // tm,),
        in_specs=[pl.BlockSpec((tm, n1), lambda i: (i, 0)),
                  pl.BlockSpec((tm, n2), lambda i: (i, 0)),
                  pl.BlockSpec((n1 + n2, d), lambda i: (0, 0), pipeline_mode=pl.Buffered(1)),
                  pl.BlockSpec((tm, d), lambda i: (i, 0))],
        out_specs=pl.BlockSpec((tm, d), lambda i: (i, 0)),
        out_shape=jax.ShapeDtypeStruct((m, d), F32),
        compiler_params=_params("parallel"),
    )(a1, a2, w, x)


def _lambda_value(lq1_ref, lk1_ref, lq2_ref, lk2_ref, lam_init):
    a = jnp.exp(jnp.sum(lq1_ref[...] * lk1_ref[...], axis=-1, keepdims=True))
    b = jnp.exp(jnp.sum(lq2_ref[...] * lk2_ref[...], axis=-1, keepdims=True))
    return a - b + lam_init


def _subln(d, gain, lam_init):
    return _rms(d, gain) * (1.0 - lam_init)


def _diff_attn_kernel(q_ref, k_ref, v_ref, lq1_ref, lk1_ref, lq2_ref, lk2_ref, g_ref, o_ref,
                      q2_ref, vt_ref, m_ref, l_ref, acc_ref, *, lam_init):
    tq = q_ref.shape[0]
    tk = vt_ref.shape[2]
    qi = pl.program_id(2)

    @pl.when(qi == 0)
    def _():
        for c in range(vt_ref.shape[0]):
            vt_ref[c] = v_ref[c * tk:(c + 1) * tk, :].T

    q = q_ref[...]
    lane = lax.broadcasted_iota(jnp.int32, q.shape, 1)
    zero = jnp.zeros_like(q)
    q2_ref[0:tq, :] = jnp.where(lane < DIFF_HEAD_DIM, q, zero)
    q2_ref[tq:2 * tq, :] = jnp.where(lane < DIFF_HEAD_DIM, zero, q)
    m_ref[...] = jnp.full(m_ref.shape, -jnp.inf, F32)
    l_ref[...] = jnp.zeros(l_ref.shape, F32)
    acc_ref[...] = jnp.zeros(acc_ref.shape, F32)

    def block(kj, masked):
        start = pl.multiple_of(kj * tk, tk)
        k = k_ref[pl.ds(start, tk), :]
        vt = vt_ref[kj]
        chains = range(0, 2 * tq, ATTN_CHAIN_LANES)
        n_keys = [(c % tq) + ATTN_CHAIN_LANES if masked else tk for c in chains]
        scores = [lax.dot_general(k[0:nk], q2_ref[c:c + ATTN_CHAIN_LANES, :], (((1,), (1,)), ((), ())),
                                  preferred_element_type=F32) for c, nk in zip(chains, n_keys)]
        for c, nk, st in zip(chains, n_keys, scores):
            cs = slice(c, c + ATTN_CHAIN_LANES)
            if masked:
                key = lax.broadcasted_iota(jnp.int32, st.shape, 0)
                qry = lax.broadcasted_iota(jnp.int32, st.shape, 1) + (c % tq)
                st = jnp.where(key <= qry, st, -jnp.inf)
            m_prev = m_ref[:, cs]
            m_next = jnp.maximum(m_prev, jnp.max(st, axis=0, keepdims=True))
            alpha = jnp.exp2(m_prev - m_next)
            p = jnp.exp2(st - m_next)
            l_ref[:, cs] = alpha * l_ref[:, cs] + jnp.sum(p, axis=0, keepdims=True)
            acc_ref[:, cs] = alpha * acc_ref[:, cs] + jnp.dot(vt[:, 0:nk], p.astype(BF16),
                                                              preferred_element_type=F32)
            m_ref[:, cs] = m_next

    def body(kj, carry):
        block(kj, False)
        return carry

    lax.fori_loop(0, qi, body, 0)
    block(qi, True)

    ot = acc_ref[...] / l_ref[...]
    lam = _lambda_value(lq1_ref, lk1_ref, lq2_ref, lk2_ref, lam_init)
    dt = ot[:, 0:tq] - lam * ot[:, tq:2 * tq]
    yt = dt * lax.rsqrt(jnp.mean(dt * dt, axis=0, keepdims=True) + NORM_EPS)
    o_ref[...] = (yt.T * g_ref[...] * (1.0 - lam_init)).astype(o_ref.dtype)


def _diff_attn_prompt(q, k, v4, lam_params, gain, *, bsz, seq, tq, lam_init):
    m = bsz * seq
    width = k.shape[1]
    heads = width // LANES
    nq = seq // tq
    small = lambda a: pl.BlockSpec(a.shape, lambda b, h, i: (0, 0))
    lam_args = [a.reshape(1, -1) for a in lam_params]
    gain = gain.reshape(1, -1)
    return pl.pallas_call(
        functools.partial(_diff_attn_kernel, lam_init=lam_init),
        grid=(bsz, heads, nq),
        in_specs=[pl.BlockSpec((tq, LANES), lambda b, h, i: (b * nq + i, h)),
                  pl.BlockSpec((seq, LANES), lambda b, h, i: (b, h)),
                  pl.BlockSpec((None, None, seq, LANES), lambda b, h, i: (b, h, 0, 0))]
                 + [small(a) for a in lam_args] + [small(gain)],
        out_specs=pl.BlockSpec((tq, LANES), lambda b, h, i: (b * nq + i, h)),
        out_shape=jax.ShapeDtypeStruct((m, width), BF16),
        scratch_shapes=[pltpu.VMEM((2 * tq, LANES), BF16),
                        pltpu.VMEM((seq // tq, LANES, tq), BF16),
                        pltpu.VMEM((1, 2 * tq), F32),
                        pltpu.VMEM((1, 2 * tq), F32),
                        pltpu.VMEM((LANES, 2 * tq), F32)],
        compiler_params=_params("parallel", "parallel", "arbitrary"),
    )(q, k, v4, *lam_args, gain)


def _decode_attn_kernel(pt_ref, q_ref, kn_ref, vn_ref, *refs, n_pages, lam_init):
    del pt_ref
    k_refs, v_refs = refs[:n_pages], refs[n_pages:2 * n_pages]
    lq1_ref, lk1_ref, lq2_ref, lk2_ref, g_ref, o_ref, qbd_ref, m_ref, l_ref, acc_ref = refs[2 * n_pages:]
    g = pl.program_id(1)
    width = q_ref.shape[1]

    @pl.when(g == 0)
    def _():
        row = lax.broadcasted_iota(jnp.int32, (DECODE_ROWS, width), 0)
        lane = lax.broadcasted_iota(jnp.int32, (DECODE_ROWS, width), 1)
        qb = jnp.broadcast_to(q_ref[...], (DECODE_ROWS, width))
        keep = (lane >= row * DIFF_HEAD_DIM) & (lane < (row + 1) * DIFF_HEAD_DIM)
        qbd_ref[...] = jnp.where(keep, qb, 0.0).astype(BF16)
        m_ref[...] = jnp.full(m_ref.shape, -jnp.inf, F32)
        l_ref[...] = jnp.zeros(l_ref.shape, F32)
        acc_ref[...] = jnp.zeros(acc_ref.shape, F32)

    qbd = qbd_ref[...]
    s = jnp.concatenate(
        [jnp.dot(qbd, kr[...].astype(BF16), preferred_element_type=F32) for kr in k_refs], axis=1)
    m_prev = m_ref[:, 0:1]
    m_next = jnp.maximum(m_prev, jnp.max(s, axis=-1, keepdims=True))
    alpha = jnp.exp(m_prev - m_next)
    p = jnp.exp(s - m_next)
    l_next = alpha * l_ref[:, 0:1] + jnp.sum(p, axis=-1, keepdims=True)
    pb = p.astype(BF16)
    page = k_refs[0].shape[1]
    for h in range(width // LANES):
        hs = slice(h * LANES, (h + 1) * LANES)
        pv = jnp.dot(pb[:, 0:page], v_refs[0][h].astype(BF16), preferred_element_type=F32)
        for r in range(1, n_pages):
            pv = pv + jnp.dot(pb[:, r * page:(r + 1) * page], v_refs[r][h].astype(BF16),
                              preferred_element_type=F32)
        acc_ref[:, hs] = alpha * acc_ref[:, hs] + pv
    m_ref[...] = jnp.broadcast_to(m_next, m_ref.shape)
    l_ref[...] = jnp.broadcast_to(l_next, l_ref.shape)

    @pl.when(g == pl.num_programs(1) - 1)
    def _():
        kn = kn_ref[...].astype(BF16).astype(F32)
        s_new = jnp.sum(qbd_ref[...].astype(F32) * kn, axis=-1, keepdims=True)
        m_old = m_ref[:, 0:1]
        m_fin = jnp.maximum(m_old, s_new)
        a = jnp.exp(m_old - m_fin)
        p_new = jnp.exp(s_new - m_fin)
        l_fin = a * l_ref[:, 0:1] + p_new
        o = (a * acc_ref[...] + p_new * vn_ref[...]) / l_fin
        lam = _lambda_value(lq1_ref, lk1_ref, lq2_ref, lk2_ref, lam_init)
        gain = g_ref[...]
        for h in range(width // LANES):
            hs = slice(h * LANES, (h + 1) * LANES)
            d = o[2 * h:2 * h + 1, hs] - lam * o[2 * h + 1:2 * h + 2, hs]
            o_ref[:, hs] = _subln(d, gain, lam_init).astype(o_ref.dtype)


def _decode_attn(q, k_new, v_new, cache_k, cache_v, page_table, lam_params, gain, *, lam_init):
    nb, _, width = q.shape
    page = cache_k.shape[2]
    n_used = page_table.shape[1]
    npg = DECODE_PAGES_PER_STEP
    assert n_used % npg == 0
    row = pl.BlockSpec((None, 1, width), lambda b, g, pt: (b, 0, 0))
    small = lambda a: pl.BlockSpec(a.shape, lambda b, g, pt: (0, 0))
    k_spec = lambda r: pl.BlockSpec((None, width, page),
                                    lambda b, g, pt: (pt[b, g * npg + r], 0, 0))
    v_spec = lambda r: pl.BlockSpec((None, width // LANES, page, LANES),
                                    lambda b, g, pt: (pt[b, g * npg + r], 0, 0, 0))
    lam_args = [a.reshape(1, -1) for a in lam_params]
    gain = gain.reshape(1, -1)
    grid_spec = pltpu.PrefetchScalarGridSpec(
        num_scalar_prefetch=1,
        grid=(nb, n_used // npg),
        in_specs=[row, row, row] + [k_spec(r) for r in range(npg)] + [v_spec(r) for r in range(npg)]
                 + [small(a) for a in lam_args] + [small(gain)],
        out_specs=pl.BlockSpec((None, 1, width), lambda b, g, pt: (b, 0, 0)),
        scratch_shapes=[pltpu.VMEM((DECODE_ROWS, width), BF16),
                        pltpu.VMEM((DECODE_ROWS, LANES), F32),
                        pltpu.VMEM((DECODE_ROWS, LANES), F32),
                        pltpu.VMEM((DECODE_ROWS, width), F32)],
    )
    return pl.pallas_call(
        functools.partial(_decode_attn_kernel, n_pages=npg, lam_init=lam_init),
        grid_spec=grid_spec,
        out_shape=jax.ShapeDtypeStruct((nb, 1, width), BF16),
        compiler_params=_params("parallel", "arbitrary"),
    )(page_table, q, k_new, v_new, *([cache_k] * npg), *([cache_v] * npg), *lam_args, gain)


def _rope_tables(pos):
    half = DIFF_HEAD_DIM // 2
    inv_freq = ROPE_THETA ** (-jnp.arange(half, dtype=F32) / half)
    ang = pos.astype(F32)[:, None] * inv_freq[None, :]
    cos, sin, zero = jnp.cos(ang), jnp.sin(ang), jnp.zeros_like(ang)
    reps = LANES // DIFF_HEAD_DIM
    cos_t = jnp.tile(jnp.concatenate([cos, cos], axis=1), (1, reps))
    sin_lo = jnp.tile(jnp.concatenate([-sin, zero], axis=1), (1, reps))
    sin_hi = jnp.tile(jnp.concatenate([zero, sin], axis=1), (1, reps))
    return cos_t, sin_lo, sin_hi


def _trunk(x, p, *, bsz, seq, tiles, rope, mem_k, mem_v, conv_state, cache, depth):
    m, d = x.shape
    n_a = depth // 2
    prompt = cache is None
    cdim = p['conv_w'].shape[-1]
    qk_width = p['w_kv'].shape[1] // 2
    mem_width = MEM_HEADS * MEM_HEAD_DIM
    tm = tiles.proj
    act = BF16 if prompt else F32
    ffn_copies = {}
    new_conv = []
    k_out = v_out = k_b = v_b = None

    def ffn(name, l, x, final_gain=None):
        res = _ffn(x, p['norm_' + name][l], p['w_' + name][l], tm=tiles.ffn, tf=tiles.ffn_cols,
                   final_gain=final_gain)
        if prompt:
            return res
        ffn_copies[(name, l)] = res[1]
        return res[0]

    for l in range(depth):
        if l == n_a:
            if prompt:
                k_out, k_b, v_out, v_b = _norm_proj(
                    x, p['norm_kv'], p['w_kv'],
                    [_Segment(qk_width, (_Out(F32, 'cols'), _Out(BF16)), rope=True),
                     _Segment(qk_width, (_Out(F32, 'heads'), _Out(BF16, 'heads')))],
                    tm=tm, seq=seq, rope=rope)
            else:
                k_out, v_out = _norm_proj(
                    x, p['norm_kv'], p['w_kv'],
                    [_Segment(qk_width, (_Out(F32),), rope=True), _Segment(qk_width, (_Out(F32),))],
                    tm=tm, seq=seq, rope=rope)
        x = ffn('ffn1', l, x)
        if l < n_a:
            proj, q_mem = _norm_proj(x, p['norm_mix'][l], p['w_in_a'][l],
                                     [_Segment(3 * cdim, (_Out(act),)), _Segment(mem_width, (_Out(act),))],
                                     tm=tm, seq=seq)
            if prompt:
                mix, st = _conv_gate_prompt(proj, p['conv_w'][l], bsz=bsz, tm=tm, cdim=cdim)
            else:
                s0, s1 = conv_state[l]
                mix, u = _conv_gate_step(proj, s0, s1, p['conv_w'][l], cdim=cdim)
                st = jnp.stack([s1, u], axis=1)
            new_conv.append(st)
        else:
            j = l - n_a
            lam_init = 0.8 - 0.6 * math.exp(-0.3 * l)
            lam_params = (p['lambda_q1'][j], p['lambda_k1'][j], p['lambda_q2'][j], p['lambda_k2'][j])
            q_scale = DIFF_HEAD_DIM ** -0.5 * (LOG2E if prompt else 1.0)
            q, q_mem = _norm_proj(
                x, p['norm_mix'][l], p['w_in_b'][j],
                [_Segment(qk_width, (_Out(act),), rope=True, scale=q_scale),
                 _Segment(mem_width, (_Out(act),))],
                tm=tm, seq=seq, rope=rope)
            if prompt:
                mix = _diff_attn_prompt(q, k_b, v_b, lam_params, p['subln_gain'][j],
                                        bsz=bsz, seq=seq, tq=tiles.attn, lam_init=lam_init)
            else:
                cache_k, cache_v, page_table = cache
                mix = _decode_attn(q.reshape(bsz, 1, qk_width),
                                   k_out.reshape(bsz, 1, qk_width), v_out.reshape(bsz, 1, qk_width),
                                   cache_k, cache_v, page_table, lam_params, p['subln_gain'][j],
                                   lam_init=lam_init).reshape(m, qk_width)
        mem_o = _mem_attn(q_mem.reshape(bsz, seq, mem_width), mem_k[l], mem_v[l], tq=min(tm, seq))
        x = _out_proj(mix, mem_o.reshape(m, mem_width), p['w_out'][l], x, tm=tm)
        x = ffn('ffn2', l, x, final_gain=p['norm_final'] if l == depth - 1 else None)
    return x, new_conv, k_out, v_out, ffn_copies


def kernel(x_prompt, x_sample, state_conv, cache_k, cache_v, cache_mem_k, cache_mem_v, page_table, mem_prompt, norm_ffn1, w_ffn1_up, w_ffn1_down, norm_mix, w_in_a, conv_w, w_in_b, lambda_q1, lambda_k1, lambda_q2, lambda_k2, subln_gain, norm_mem, w_mem_kv, w_out, norm_ffn2, w_ffn2_up, w_ffn2_down, norm_kv, w_kv, norm_final):
    bsz_p, s_p, d = x_prompt.shape
    bsz_s, s_s, _ = x_sample.shape
    depth = norm_ffn1.shape[0]
    n_mem = mem_prompt.shape[1]
    mem_width = MEM_HEADS * MEM_HEAD_DIM
    n_pool, page = cache_k.shape[0], cache_k.shape[1]
    k_heads, v_heads = cache_k.shape[2], cache_v.shape[2]
    qk_width = k_heads * cache_k.shape[3]
    past_len = page_table.shape[1] * page
    assert s_s == 1

    def per_layer(w):
        return [_cast_layer(w, l) for l in range(w.shape[0])]

    p = {
        'norm_ffn1': norm_ffn1, 'norm_ffn2': norm_ffn2,
        'norm_mix': norm_mix, 'w_in_a': per_layer(w_in_a), 'conv_w': conv_w, 'w_in_b': per_layer(w_in_b),
        'lambda_q1': lambda_q1, 'lambda_k1': lambda_k1, 'lambda_q2': lambda_q2, 'lambda_k2': lambda_k2,
        'subln_gain': subln_gain, 'w_out': per_layer(w_out),
        'norm_kv': norm_kv, 'w_kv': _cast_layer(w_kv[None], 0), 'norm_final': norm_final,
    }
    w_mem_kv_b = per_layer(w_mem_kv)

    rows_s = bsz_s * s_s
    pos_s = jnp.full((rows_s,), past_len, jnp.int32)
    cmk = cache_mem_k.reshape(depth, bsz_s, n_mem, mem_width)
    cmv = cache_mem_v.reshape(depth, bsz_s, n_mem, mem_width)
    p_sample = dict(p, w_ffn1=[(w_ffn1_up, w_ffn1_down, l) for l in range(depth)],
                    w_ffn2=[(w_ffn2_up, w_ffn2_down, l) for l in range(depth)])
    y_s, conv_s, k_s, v_s, copies = _trunk(
        x_sample.reshape(rows_s, d), p_sample, bsz=bsz_s, seq=s_s,
        tiles=_Tiles(proj=rows_s, ffn=rows_s, ffn_cols=PROMPT_TILES.ffn_cols, attn=0),
        rope=_rope_tables(pos_s),
        mem_k=[cmk[l] for l in range(depth)], mem_v=[cmv[l] for l in range(depth)],
        conv_state=[(state_conv[l, :, 0], state_conv[l, :, 1]) for l in range(depth // 2)],
        cache=(jnp.transpose(cache_k, (0, 2, 3, 1)).reshape(n_pool, qk_width, page),
               jnp.transpose(cache_v, (0, 2, 1, 3)), page_table),
        depth=depth)

    mem_rows = mem_prompt.reshape(bsz_p * n_mem, d)
    mem_kv = [_norm_proj(mem_rows, norm_mem[l], w_mem_kv_b[l],
                         [_Segment(mem_width, (_Out(F32),)), _Segment(mem_width, (_Out(F32),))],
                         tm=n_mem, seq=n_mem)
              for l in range(depth)]
    mem_k_p = [kv[0].reshape(bsz_p, n_mem, mem_width) for kv in mem_kv]
    mem_v_p = [kv[1].reshape(bsz_p, n_mem, mem_width) for kv in mem_kv]
    p_prompt = dict(p, w_ffn1=[copies[('ffn1', l)] for l in range(depth)],
                    w_ffn2=[copies[('ffn2', l)] for l in range(depth)])
    y_p, conv_p, k_t, v_h, _ = _trunk(
        x_prompt.reshape(bsz_p * s_p, d), p_prompt, bsz=bsz_p, seq=s_p, tiles=PROMPT_TILES,
        rope=_rope_tables(jnp.arange(s_p)), mem_k=mem_k_p, mem_v=mem_v_p,
        conv_state=None, cache=None, depth=depth)

    mem_shape = (depth, bsz_p, n_mem, MEM_HEADS, MEM_HEAD_DIM)
    return (y_p.reshape(bsz_p, s_p, d),
            y_s.reshape(bsz_s, s_s, d),
            jnp.stack(conv_p, axis=0),
            jnp.stack(conv_s, axis=0),
            jnp.transpose(k_t.reshape(bsz_p, k_heads, -1, s_p), (0, 3, 1, 2)),
            jnp.transpose(v_h, (0, 2, 1, 3)),
            k_s.reshape(bsz_s, s_s, k_heads, -1),
            v_s.reshape(bsz_s, s_s, v_heads, -1),
            jnp.stack(mem_k_p, axis=0).reshape(mem_shape),
            jnp.stack(mem_v_p, axis=0).reshape(mem_shape))
```

```python
import functools
import math
from typing import NamedTuple

import jax
import jax.numpy as jnp
from jax import lax
from jax.experimental import pallas as pl
from jax.experimental.pallas import tpu as pltpu

F32 = jnp.float32
BF16 = jnp.bfloat16

NORM_EPS = 1e-6
ROPE_THETA = 10000.0
LOG2E = math.log2(math.e)
LANES = 128
VMEM_LIMIT = 56 * 1024 * 1024
PROJ_CHUNK = 512
CAST_BLOCK_BYTES = 8 * 1024 * 1024

MEM_HEADS = 4
MEM_HEAD_DIM = 128
DIFF_HEAD_DIM = 64
CONV_WIDTH = 3
ATTN_CHAIN_LANES = 256
DECODE_PAGES_PER_STEP = 8
DECODE_ROWS = 32


class _Tiles(NamedTuple):
    proj: int
    ffn: int
    ffn_cols: int
    attn: int


PROMPT_TILES = _Tiles(proj=512, ffn=1024, ffn_cols=512, attn=512)


def _params(*sem):
    return pltpu.CompilerParams(dimension_semantics=sem, vmem_limit_bytes=VMEM_LIMIT)


def _rms(x, g):
    ms = jnp.mean(x * x, axis=-1, keepdims=True)
    return x * lax.rsqrt(ms + NORM_EPS) * g


def _cast_kernel(w_ref, o_ref):
    o_ref[...] = w_ref[...].astype(o_ref.dtype)


def _cast_layer(w3, layer):
    _, k, n = w3.shape
    rows = k
    while rows * n * 4 > CAST_BLOCK_BYTES and rows % 16 == 0:
        rows //= 2
    return pl.pallas_call(
        _cast_kernel,
        grid=(k // rows,),
        in_specs=[pl.BlockSpec((None, rows, n), lambda i: (layer, i, 0))],
        out_specs=pl.BlockSpec((rows, n), lambda i: (i, 0)),
        out_shape=jax.ShapeDtypeStruct((k, n), BF16),
        compiler_params=_params("parallel"),
    )(w3)


def _rope_tile(y, cos, sin_lo, sin_hi):
    outs = []
    for c in range(y.shape[1] // LANES):
        yc = y[:, c * LANES:(c + 1) * LANES]
        up = pltpu.roll(yc, LANES - 32, axis=1)
        dn = pltpu.roll(yc, 32, axis=1)
        outs.append(yc * cos + up * sin_lo + dn * sin_hi)
    return jnp.concatenate(outs, axis=1) if len(outs) > 1 else outs[0]


class _Out(NamedTuple):
    dtype: object
    layout: str = 'rows'


class _Segment(NamedTuple):
    n_cols: int
    outs: tuple
    rope: bool = False
    scale: float = 1.0


def _norm_proj_kernel(*refs, segments, has_rope):
    if has_rope:
        x_ref, g_ref, w_ref, cos_ref, slo_ref, shi_ref = refs[:6]
        out_refs = refs[6:]
    else:
        x_ref, g_ref, w_ref = refs[:3]
        out_refs = refs[3:]
    h = _rms(x_ref[...], g_ref[...]).astype(BF16)
    col, k = 0, 0
    for seg in segments:
        outs = out_refs[k:k + len(seg.outs)]
        for c0 in range(0, seg.n_cols, PROJ_CHUNK):
            c1 = min(c0 + PROJ_CHUNK, seg.n_cols)
            y = jnp.dot(h, w_ref[:, col + c0:col + c1], preferred_element_type=F32)
            if seg.rope:
                y = _rope_tile(y, cos_ref[...], slo_ref[...], shi_ref[...]) * seg.scale
            for o, spec in zip(outs, seg.outs):
                if spec.layout == 'rows':
                    o[:, c0:c1] = y.astype(o.dtype)
                elif spec.layout == 'cols':
                    o[c0:c1, :] = y.T.astype(o.dtype)
                else:
                    for hh in range(c0 // LANES, c1 // LANES):
                        o[hh] = y[:, hh * LANES - c0:(hh + 1) * LANES - c0].astype(o.dtype)
        col += seg.n_cols
        k += len(seg.outs)


def _norm_proj(x, g, w, segments, *, tm, seq, rope=None):
    m, d = x.shape
    assert m % tm == 0 and sum(s.n_cols for s in segments) == w.shape[1]
    has_rope = any(s.rope for s in segments)
    row_major = all(o.layout == 'rows' for s in segments for o in s.outs)
    assert row_major or seq % tm == 0
    per_seq = max(seq // tm, 1)
    in_specs = [
        pl.BlockSpec((tm, d), lambda i: (i, 0)),
        pl.BlockSpec((1, d), lambda i: (0, 0)),
        pl.BlockSpec(w.shape, lambda i: (0, 0), pipeline_mode=pl.Buffered(1)),
    ]
    args = [x, g.reshape(1, d), w]
    if has_rope:
        n_tab = rope[0].shape[0] // tm
        for t in rope:
            in_specs.append(pl.BlockSpec((tm, LANES), lambda i: (i % n_tab, 0)))
            args.append(t)
    out_specs, out_shape = [], []
    for seg in segments:
        n = seg.n_cols
        for o in seg.outs:
            if o.layout == 'rows':
                out_specs.append(pl.BlockSpec((tm, n), lambda i: (i, 0)))
                out_shape.append(jax.ShapeDtypeStruct((m, n), o.dtype))
            elif o.layout == 'cols':
                out_specs.append(pl.BlockSpec((None, n, tm), lambda i: (i // per_seq, 0, i % per_seq)))
                out_shape.append(jax.ShapeDtypeStruct((m // seq, n, seq), o.dtype))
            else:
                out_specs.append(pl.BlockSpec((None, n // LANES, tm, LANES),
                                              lambda i: (i // per_seq, 0, i % per_seq, 0)))
                out_shape.append(jax.ShapeDtypeStruct((m // seq, n // LANES, seq, LANES), o.dtype))
    return pl.pallas_call(
        functools.partial(_norm_proj_kernel, segments=tuple(segments), has_rope=has_rope),
        grid=(m // tm,),
        in_specs=in_specs,
        out_specs=out_specs,
        out_shape=out_shape,
        compiler_params=_params("parallel"),
    )(*args)


def _ffn_kernel(*refs, final_norm, emit_weights):
    x_ref, g_ref, wg_ref, wu_ref, wd_ref = refs[:5]
    refs = refs[5:]
    if final_norm:
        gf_ref, refs = refs[0], refs[1:]
    o_ref, refs = refs[0], refs[1:]
    if emit_weights:
        wg_out, wu_out, wd_out = refs[:3]
        refs = refs[3:]
    h_ref, = refs
    j = pl.program_id(1)

    @pl.when(j == 0)
    def _():
        x = x_ref[...]
        h_ref[...] = _rms(x, g_ref[...]).astype(BF16)
        o_ref[...] = x

    wg = wg_ref[...].astype(BF16)
    wu = wu_ref[...].astype(BF16)
    wd = wd_ref[...].astype(BF16)
    if emit_weights:
        wg_out[...] = wg
        wu_out[...] = wu
        wd_out[...] = wd
    h = h_ref[...]
    gate = jnp.dot(h, wg, preferred_element_type=F32)
    up = jnp.dot(h, wu, preferred_element_type=F32)
    act = (gate * jax.nn.sigmoid(gate) * up * 0.5).astype(BF16)
    o_ref[...] += jnp.dot(act, wd, preferred_element_type=F32)

    if final_norm:
        @pl.when(j == pl.num_programs(1) - 1)
        def _():
            o_ref[...] = _rms(o_ref[...], gf_ref[...])


def _ffn(x, g, weights, *, tm, tf, final_gain=None):
    m, d = x.shape
    emit = isinstance(weights[2], int)
    if emit:
        w_up3, w_down3, layer = weights
        f = w_down3.shape[1]
        nf = f // tf
        w_specs = [pl.BlockSpec((None, d, tf), lambda i, j: (layer, 0, j)),
                   pl.BlockSpec((None, d, tf), lambda i, j: (layer, 0, nf + j)),
                   pl.BlockSpec((None, tf, d), lambda i, j: (layer, j, 0))]
        w_args = [w_up3, w_up3, w_down3]
    else:
        f = weights[2].shape[0]
        nf = f // tf
        w_specs = [pl.BlockSpec((d, tf), lambda i, j: (0, j)),
                   pl.BlockSpec((d, tf), lambda i, j: (0, j)),
                   pl.BlockSpec((tf, d), lambda i, j: (j, 0))]
        w_args = list(weights)
    assert m % tm == 0 and f % tf == 0
    in_specs = [
        pl.BlockSpec((tm, d), lambda i, j: (i, 0), pipeline_mode=pl.Buffered(1)),
        pl.BlockSpec((1, d), lambda i, j: (0, 0)),
    ] + w_specs
    args = [x, g.reshape(1, d)] + w_args
    if final_gain is not None:
        in_specs.append(pl.BlockSpec((1, d), lambda i, j: (0, 0)))
        args.append(final_gain.reshape(1, d))
    out_specs = [pl.BlockSpec((tm, d), lambda i, j: (i, 0))]
    out_shape = [jax.ShapeDtypeStruct((m, d), F32)]
    if emit:
        assert m == tm
        out_specs += [pl.BlockSpec((d, tf), lambda i, j: (0, j)),
                      pl.BlockSpec((d, tf), lambda i, j: (0, j)),
                      pl.BlockSpec((tf, d), lambda i, j: (j, 0))]
        out_shape += [jax.ShapeDtypeStruct((d, f), BF16), jax.ShapeDtypeStruct((d, f), BF16),
                      jax.ShapeDtypeStruct((f, d), BF16)]
    outs = pl.pallas_call(
        functools.partial(_ffn_kernel, final_norm=final_gain is not None, emit_weights=emit),
        grid=(m // tm, nf),
        in_specs=in_specs,
        out_specs=out_specs,
        out_shape=out_shape,
        scratch_shapes=[pltpu.VMEM((tm, d), BF16)],
        compiler_params=_params("parallel", "arbitrary"),
    )(*args)
    return (outs[0], tuple(outs[1:])) if emit else outs[0]


def _conv_mixer_kernel(x_ref, g_ref, w_ref, cw_ref, mix_ref, qmem_ref, st_ref, carry_ref, ubuf,
                       *, tiles_per_seq):
    tm = x_ref.shape[0]
    cdim = mix_ref.shape[1]
    chunk = ubuf.shape[1]
    i = pl.program_id(0)

    @pl.when(i % tiles_per_seq == 0)
    def _():
        carry_ref[...] = jnp.zeros(carry_ref.shape, F32)

    h = _rms(x_ref[...], g_ref[...]).astype(BF16)
    cw = cw_ref[...]
    for c0 in range(0, cdim, chunk):
        cs = slice(c0, c0 + chunk)
        b_gate = jnp.dot(h, w_ref[:, c0:c0 + chunk], preferred_element_type=F32)
        c_gate = jnp.dot(h, w_ref[:, cdim + c0:cdim + c0 + chunk], preferred_element_type=F32)
        x_in = jnp.dot(h, w_ref[:, 2 * cdim + c0:2 * cdim + c0 + chunk], preferred_element_type=F32)
        u = c_gate * x_in
        ubuf[0:8, :] = carry_ref[:, cs]
        ubuf[8:tm + 8, :] = u
        conv = cw[0:1, cs] * ubuf[6:tm + 6, :] + cw[1:2, cs] * ubuf[7:tm + 7, :] + cw[2:3, cs] * u
        mix_ref[:, cs] = (b_gate * conv).astype(mix_ref.dtype)
        st_ref[:, cs] = u[tm - (CONV_WIDTH - 1):tm]
        carry_ref[:, cs] = u[tm - 8:tm]
    qmem_ref[...] = jnp.dot(h, w_ref[:, 3 * cdim:], preferred_element_type=F32).astype(qmem_ref.dtype)


def _conv_mixer_prompt(x, g, w, conv_w, *, bsz, tm, cdim):
    m, d = x.shape
    tiles_per_seq = m // bsz // tm
    n_mem_q = w.shape[1] - 3 * cdim
    return pl.pallas_call(
        functools.partial(_conv_mixer_kernel, tiles_per_seq=tiles_per_seq),
        grid=(m // tm,),
        in_specs=[pl.BlockSpec((tm, d), lambda i: (i, 0)),
                  pl.BlockSpec((1, d), lambda i: (0, 0)),
                  pl.BlockSpec(w.shape, lambda i: (0, 0), pipeline_mode=pl.Buffered(1)),
                  pl.BlockSpec((CONV_WIDTH, cdim), lambda i: (0, 0))],
        out_specs=[pl.BlockSpec((tm, cdim), lambda i: (i, 0)),
                   pl.BlockSpec((tm, n_mem_q), lambda i: (i, 0)),
                   pl.BlockSpec((None, CONV_WIDTH - 1, cdim), lambda i: (i // tiles_per_seq, 0, 0))],
        out_shape=[jax.ShapeDtypeStruct((m, cdim), BF16),
                   jax.ShapeDtypeStruct((m, n_mem_q), BF16),
                   jax.ShapeDtypeStruct((bsz, CONV_WIDTH - 1, cdim), F32)],
        scratch_shapes=[pltpu.VMEM((8, cdim), F32), pltpu.VMEM((tm + 8, PROJ_CHUNK), F32)],
        compiler_params=_params("arbitrary"),
    )(x, g.reshape(1, d), w, conv_w)


def _conv_gate_step_kernel(b_ref, c_ref, xin_ref, s0_ref, s1_ref, w_ref, mix_ref, u_ref):
    u = c_ref[...].astype(F32) * xin_ref[...].astype(F32)
    w = w_ref[...]
    conv = w[0:1] * s0_ref[...] + w[1:2] * s1_ref[...] + w[2:3] * u
    mix_ref[...] = (b_ref[...].astype(F32) * conv).astype(mix_ref.dtype)
    u_ref[...] = u


def _conv_gate_step(proj, s0, s1, conv_w, *, cdim):
    m = proj.shape[0]
    col = lambda cb: pl.BlockSpec((m, cdim), lambda i: (0, cb))
    full = lambda r: pl.BlockSpec((r, cdim), lambda i: (0, 0))
    return pl.pallas_call(
        _conv_gate_step_kernel,
        grid=(1,),
        in_specs=[col(0), col(1), col(2), full(m), full(m), full(CONV_WIDTH)],
        out_specs=[full(m), full(m)],
        out_shape=[jax.ShapeDtypeStruct((m, cdim), BF16), jax.ShapeDtypeStruct((m, cdim), F32)],
        compiler_params=_params("arbitrary"),
    )(proj, proj, proj, s0, s1, conv_w)


def _mem_attn_kernel(q_ref, k_ref, v_ref, o_ref):
    rows = q_ref.shape[0]
    q = q_ref[...].astype(BF16)
    if rows < 8:
        q = jnp.broadcast_to(q[0:1], (8, q.shape[1]))
    k = k_ref[...].astype(BF16)
    v = v_ref[...].astype(BF16)
    scale = MEM_HEAD_DIM ** -0.5
    for h in range(MEM_HEADS):
        hs = slice(h * MEM_HEAD_DIM, (h + 1) * MEM_HEAD_DIM)
        s = lax.dot_general(q[:, hs], k[:, hs], (((1,), (1,)), ((), ())),
                            preferred_element_type=F32) * scale
        p = jnp.exp(s - jnp.max(s, axis=-1, keepdims=True))
        l = jnp.sum(p, axis=-1, keepdims=True)
        o = jnp.dot(p.astype(BF16), v[:, hs], preferred_element_type=F32) / l
        o_ref[:, hs] = o[:rows].astype(o_ref.dtype)


def _mem_attn(q3, k3, v3, *, tq):
    nb, rows, width = q3.shape
    n_mem = k3.shape[1]
    return pl.pallas_call(
        _mem_attn_kernel,
        grid=(nb, rows // tq),
        in_specs=[pl.BlockSpec((None, tq, width), lambda b, i: (b, i, 0)),
                  pl.BlockSpec((None, n_mem, width), lambda b, i: (b, 0, 0)),
                  pl.BlockSpec((None, n_mem, width), lambda b, i: (b, 0, 0))],
        out_specs=pl.BlockSpec((None, tq, width), lambda b, i: (b, i, 0)),
        out_shape=jax.ShapeDtypeStruct((nb, rows, width), BF16),
        compiler_params=_params("parallel", "parallel"),
    )(q3, k3, v3)


def _out_proj_kernel(a1_ref, a2_ref, w_ref, x_ref, o_ref):
    n1 = a1_ref.shape[1]
    acc = jnp.dot(a1_ref[...], w_ref[0:n1, :], preferred_element_type=F32)
    acc = acc + jnp.dot(a2_ref[...], w_ref[n1:, :], preferred_element_type=F32)
    o_ref[...] = x_ref[...] + acc


def _out_proj(a1, a2, w, x, *, tm):
    m, d = x.shape
    n1, n2 = a1.shape[1], a2.shape[1]
    return pl.pallas_call(
        _out_proj_kernel,
        grid=(m // tm,),
        in_specs=[pl.BlockSpec((tm, n1), lambda i: (i, 0)),
                  pl.BlockSpec((tm, n2), lambda i: (i, 0)),
                  pl.BlockSpec((n1 + n2, d), lambda i: (0, 0), pipeline_mode=pl.Buffered(1)),
                  pl.BlockSpec((tm, d), lambda i: (i, 0))],
        out_specs=pl.BlockSpec((tm, d), lambda i: (i, 0)),
        out_shape=jax.ShapeDtypeStruct((m, d), F32),
        compiler_params=_params("parallel"),
    )(a1, a2, w, x)


def _lambda_value(lq1_ref, lk1_ref, lq2_ref, lk2_ref, lam_init):
    a = jnp.exp(jnp.sum(lq1_ref[...] * lk1_ref[...], axis=-1, keepdims=True))
    b = jnp.exp(jnp.sum(lq2_ref[...] * lk2_ref[...], axis=-1, keepdims=True))
    return a - b + lam_init


def _subln(d, gain, lam_init):
    return _rms(d, gain) * (1.0 - lam_init)


def _diff_attn_kernel(q_ref, k_ref, v_ref, lq1_ref, lk1_ref, lq2_ref, lk2_ref, g_ref, o_ref,
                      q2_ref, vt_ref, m_ref, l_ref, acc_ref, *, lam_init):
    tq = q_ref.shape[0]
    tk = vt_ref.shape[2]
    qi = pl.program_id(2)

    @pl.when(qi == 0)
    def _():
        for c in range(vt_ref.shape[0]):
            vt_ref[c] = v_ref[c * tk:(c + 1) * tk, :].T

    q = q_ref[...]
    lane = lax.broadcasted_iota(jnp.int32, q.shape, 1)
    zero = jnp.zeros_like(q)
    q2_ref[0:tq, :] = jnp.where(lane < DIFF_HEAD_DIM, q, zero)
    q2_ref[tq:2 * tq, :] = jnp.where(lane < DIFF_HEAD_DIM, zero, q)
    m_ref[...] = jnp.full(m_ref.shape, -jnp.inf, F32)
    l_ref[...] = jnp.zeros(l_ref.shape, F32)
    acc_ref[...] = jnp.zeros(acc_ref.shape, F32)

    chains = range(0, 2 * tq, ATTN_CHAIN_LANES)

    def n_keys(c, masked):
        return (c % tq) + ATTN_CHAIN_LANES if masked else tk

    def scores(kj, masked):
        start = pl.multiple_of(kj * tk, tk)
        k = k_ref[pl.ds(start, tk), :]
        return [lax.dot_general(k[0:n_keys(c, masked)], q2_ref[c:c + ATTN_CHAIN_LANES, :],
                                (((1,), (1,)), ((), ())), preferred_element_type=F32) for c in chains]

    def absorb(kj, sts, masked):
        vt = vt_ref[kj]
        for c, st in zip(chains, sts):
            cs = slice(c, c + ATTN_CHAIN_LANES)
            if masked:
                key = lax.broadcasted_iota(jnp.int32, st.shape, 0)
                qry = lax.broadcasted_iota(jnp.int32, st.shape, 1) + (c % tq)
                st = jnp.where(key <= qry, st, -jnp.inf)
            m_prev = m_ref[:, cs]
            m_next = jnp.maximum(m_prev, jnp.max(st, axis=0, keepdims=True))
            alpha = jnp.exp2(m_prev - m_next)
            p = jnp.exp2(st - m_next)
            l_ref[:, cs] = alpha * l_ref[:, cs] + jnp.sum(p, axis=0, keepdims=True)
            acc_ref[:, cs] = alpha * acc_ref[:, cs] + jnp.dot(vt[:, 0:n_keys(c, masked)], p.astype(BF16),
                                                              preferred_element_type=F32)
            m_ref[:, cs] = m_next

    def pair(t, carry):
        sa, sb = scores(2 * t, False), scores(2 * t + 1, False)
        absorb(2 * t, sa, False)
        absorb(2 * t + 1, sb, False)
        return carry

    lax.fori_loop(0, qi // 2, pair, 0)

    @pl.when(qi % 2 == 0)
    def _():
        absorb(qi, scores(qi, True), True)

    @pl.when(qi % 2 == 1)
    def _():
        sa, sd = scores(qi - 1, False), scores(qi, True)
        absorb(qi - 1, sa, False)
        absorb(qi, sd, True)

    ot = acc_ref[...] / l_ref[...]
    lam = _lambda_value(lq1_ref, lk1_ref, lq2_ref, lk2_ref, lam_init)
    dt = ot[:, 0:tq] - lam * ot[:, tq:2 * tq]
    yt = dt * lax.rsqrt(jnp.mean(dt * dt, axis=0, keepdims=True) + NORM_EPS)
    o_ref[...] = (yt.T * g_ref[...] * (1.0 - lam_init)).astype(o_ref.dtype)


def _diff_attn_prompt(q, k, v4, lam_params, gain, *, bsz, seq, tq, lam_init):
    m = bsz * seq
    width = k.shape[1]
    heads = width // LANES
    nq = seq // tq
    small = lambda a: pl.BlockSpec(a.shape, lambda b, h, i: (0, 0))
    lam_args = [a.reshape(1, -1) for a in lam_params]
    gain = gain.reshape(1, -1)
    return pl.pallas_call(
        functools.partial(_diff_attn_kernel, lam_init=lam_init),
        grid=(bsz, heads, nq),
        in_specs=[pl.BlockSpec((tq, LANES), lambda b, h, i: (b * nq + i, h)),
                  pl.BlockSpec((seq, LANES), lambda b, h, i: (b, h)),
                  pl.BlockSpec((None, None, seq, LANES), lambda b, h, i: (b, h, 0, 0))]
                 + [small(a) for a in lam_args] + [small(gain)],
        out_specs=pl.BlockSpec((tq, LANES), lambda b, h, i: (b * nq + i, h)),
        out_shape=jax.ShapeDtypeStruct((m, width), BF16),
        scratch_shapes=[pltpu.VMEM((2 * tq, LANES), BF16),
                        pltpu.VMEM((seq // tq, LANES, tq), BF16),
                        pltpu.VMEM((1, 2 * tq), F32),
                        pltpu.VMEM((1, 2 * tq), F32),
                        pltpu.VMEM((LANES, 2 * tq), F32)],
        compiler_params=_params("parallel", "parallel", "arbitrary"),
    )(q, k, v4, *lam_args, gain)


def _decode_attn_kernel(pt_ref, q_ref, kn_ref, vn_ref, *refs, n_pages, lam_init):
    del pt_ref
    k_refs, v_refs = refs[:n_pages], refs[n_pages:2 * n_pages]
    lq1_ref, lk1_ref, lq2_ref, lk2_ref, g_ref, o_ref, qbd_ref, m_ref, l_ref, acc_ref = refs[2 * n_pages:]
    g = pl.program_id(1)
    width = q_ref.shape[1]

    @pl.when(g == 0)
    def _():
        row = lax.broadcasted_iota(jnp.int32, (DECODE_ROWS, width), 0)
        lane = lax.broadcasted_iota(jnp.int32, (DECODE_ROWS, width), 1)
        qb = jnp.broadcast_to(q_ref[...], (DECODE_ROWS, width))
        keep = (lane >= row * DIFF_HEAD_DIM) & (lane < (row + 1) * DIFF_HEAD_DIM)
        qbd_ref[...] = jnp.where(keep, qb, 0.0).astype(BF16)
        m_ref[...] = jnp.full(m_ref.shape, -jnp.inf, F32)
        l_ref[...] = jnp.zeros(l_ref.shape, F32)
        acc_ref[...] = jnp.zeros(acc_ref.shape, F32)

    qbd = qbd_ref[...]
    s = jnp.concatenate(
        [jnp.dot(qbd, kr[...].astype(BF16), preferred_element_type=F32) for kr in k_refs], axis=1)
    m_prev = m_ref[:, 0:1]
    m_next = jnp.maximum(m_prev, jnp.max(s, axis=-1, keepdims=True))
    alpha = jnp.exp(m_prev - m_next)
    p = jnp.exp(s - m_next)
    l_next = alpha * l_ref[:, 0:1] + jnp.sum(p, axis=-1, keepdims=True)
    pb = p.astype(BF16)
    page = k_refs[0].shape[1]
    for h in range(width // LANES):
        hs = slice(h * LANES, (h + 1) * LANES)
        pv = jnp.dot(pb[:, 0:page], v_refs[0][h].astype(BF16), preferred_element_type=F32)
        for r in range(1, n_pages):
            pv = pv + jnp.dot(pb[:, r * page:(r + 1) * page], v_refs[r][h].astype(BF16),
                              preferred_element_type=F32)
        acc_ref[:, hs] = alpha * acc_ref[:, hs] + pv
    m_ref[...] = jnp.broadcast_to(m_next, m_ref.shape)
    l_ref[...] = jnp.broadcast_to(l_next, l_ref.shape)

    @pl.when(g == pl.num_programs(1) - 1)
    def _():
        kn = kn_ref[...].astype(BF16).astype(F32)
        s_new = jnp.sum(qbd_ref[...].astype(F32) * kn, axis=-1, keepdims=True)
        m_old = m_ref[:, 0:1]
        m_fin = jnp.maximum(m_old, s_new)
        a = jnp.exp(m_old - m_fin)
        p_new = jnp.exp(s_new - m_fin)
        l_fin = a * l_ref[:, 0:1] + p_new
        o = (a * acc_ref[...] + p_new * vn_ref[...]) / l_fin
        lam = _lambda_value(lq1_ref, lk1_ref, lq2_ref, lk2_ref, lam_init)
        gain = g_ref[...]
        for h in range(width // LANES):
            hs = slice(h * LANES, (h + 1) * LANES)
            d = o[2 * h:2 * h + 1, hs] - lam * o[2 * h + 1:2 * h + 2, hs]
            o_ref[:, hs] = _subln(d, gain, lam_init).astype(o_ref.dtype)


def _decode_attn(q, k_new, v_new, cache_k, cache_v, page_table, lam_params, gain, *, lam_init):
    nb, _, width = q.shape
    page = cache_k.shape[2]
    n_used = page_table.shape[1]
    npg = DECODE_PAGES_PER_STEP
    assert n_used % npg == 0
    row = pl.BlockSpec((None, 1, width), lambda b, g, pt: (b, 0, 0))
    small = lambda a: pl.BlockSpec(a.shape, lambda b, g, pt: (0, 0))
    k_spec = lambda r: pl.BlockSpec((None, width, page),
                                    lambda b, g, pt: (pt[b, g * npg + r], 0, 0))
    v_spec = lambda r: pl.BlockSpec((None, width // LANES, page, LANES),
                                    lambda b, g, pt: (pt[b, g * npg + r], 0, 0, 0))
    lam_args = [a.reshape(1, -1) for a in lam_params]
    gain = gain.reshape(1, -1)
    grid_spec = pltpu.PrefetchScalarGridSpec(
        num_scalar_prefetch=1,
        grid=(nb, n_used // npg),
        in_specs=[row, row, row] + [k_spec(r) for r in range(npg)] + [v_spec(r) for r in range(npg)]
                 + [small(a) for a in lam_args] + [small(gain)],
        out_specs=pl.BlockSpec((None, 1, width), lambda b, g, pt: (b, 0, 0)),
        scratch_shapes=[pltpu.VMEM((DECODE_ROWS, width), BF16),
                        pltpu.VMEM((DECODE_ROWS, LANES), F32),
                        pltpu.VMEM((DECODE_ROWS, LANES), F32),
                        pltpu.VMEM((DECODE_ROWS, width), F32)],
    )
    return pl.pallas_call(
        functools.partial(_decode_attn_kernel, n_pages=npg, lam_init=lam_init),
        grid_spec=grid_spec,
        out_shape=jax.ShapeDtypeStruct((nb, 1, width), BF16),
        compiler_params=_params("parallel", "arbitrary"),
    )(page_table, q, k_new, v_new, *([cache_k] * npg), *([cache_v] * npg), *lam_args, gain)


def _rope_tables(pos):
    half = DIFF_HEAD_DIM // 2
    inv_freq = ROPE_THETA ** (-jnp.arange(half, dtype=F32) / half)
    ang = pos.astype(F32)[:, None] * inv_freq[None, :]
    cos, sin, zero = jnp.cos(ang), jnp.sin(ang), jnp.zeros_like(ang)
    reps = LANES // DIFF_HEAD_DIM
    cos_t = jnp.tile(jnp.concatenate([cos, cos], axis=1), (1, reps))
    sin_lo = jnp.tile(jnp.concatenate([-sin, zero], axis=1), (1, reps))
    sin_hi = jnp.tile(jnp.concatenate([zero, sin], axis=1), (1, reps))
    return cos_t, sin_lo, sin_hi


def _trunk(x, p, *, bsz, seq, tiles, rope, mem_k, mem_v, conv_state, cache, depth):
    m, d = x.shape
    n_a = depth // 2
    prompt = cache is None
    cdim = p['conv_w'].shape[-1]
    qk_width = p['w_kv'].shape[1] // 2
    mem_width = MEM_HEADS * MEM_HEAD_DIM
    tm = tiles.proj
    act = BF16 if prompt else F32
    ffn_copies = {}
    new_conv = []
    k_out = v_out = k_b = v_b = None

    def ffn(name, l, x, final_gain=None):
        res = _ffn(x, p['norm_' + name][l], p['w_' + name][l], tm=tiles.ffn, tf=tiles.ffn_cols,
                   final_gain=final_gain)
        if prompt:
            return res
        ffn_copies[(name, l)] = res[1]
        return res[0]

    for l in range(depth):
        if l == n_a:
            if prompt:
                k_out, k_b, v_out, v_b = _norm_proj(
                    x, p['norm_kv'], p['w_kv'],
                    [_Segment(qk_width, (_Out(F32, 'cols'), _Out(BF16)), rope=True),
                     _Segment(qk_width, (_Out(F32, 'heads'), _Out(BF16, 'heads')))],
                    tm=tm, seq=seq, rope=rope)
            else:
                k_out, v_out = _norm_proj(
                    x, p['norm_kv'], p['w_kv'],
                    [_Segment(qk_width, (_Out(F32),), rope=True), _Segment(qk_width, (_Out(F32),))],
                    tm=tm, seq=seq, rope=rope)
        x = ffn('ffn1', l, x)
        if l < n_a:
            if prompt:
                mix, q_mem, st = _conv_mixer_prompt(x, p['norm_mix'][l], p['w_in_a'][l], p['conv_w'][l],
                                                    bsz=bsz, tm=tm, cdim=cdim)
            else:
                proj, q_mem = _norm_proj(x, p['norm_mix'][l], p['w_in_a'][l],
                                         [_Segment(3 * cdim, (_Out(act),)), _Segment(mem_width, (_Out(act),))],
                                         tm=tm, seq=seq)
                s0, s1 = conv_state[l]
                mix, u = _conv_gate_step(proj, s0, s1, p['conv_w'][l], cdim=cdim)
                st = jnp.stack([s1, u], axis=1)
            new_conv.append(st)
        else:
            j = l - n_a
            lam_init = 0.8 - 0.6 * math.exp(-0.3 * l)
            lam_params = (p['lambda_q1'][j], p['lambda_k1'][j], p['lambda_q2'][j], p['lambda_k2'][j])
            q_scale = DIFF_HEAD_DIM ** -0.5 * (LOG2E if prompt else 1.0)
            q, q_mem = _norm_proj(
                x, p['norm_mix'][l], p['w_in_b'][j],
                [_Segment(qk_width, (_Out(act),), rope=True, scale=q_scale),
                 _Segment(mem_width, (_Out(act),))],
                tm=tm, seq=seq, rope=rope)
            if prompt:
                mix = _diff_attn_prompt(q, k_b, v_b, lam_params, p['subln_gain'][j],
                                        bsz=bsz, seq=seq, tq=tiles.attn, lam_init=lam_init)
            else:
                cache_k, cache_v, page_table = cache
                mix = _decode_attn(q.reshape(bsz, 1, qk_width),
                                   k_out.reshape(bsz, 1, qk_width), v_out.reshape(bsz, 1, qk_width),
                                   cache_k, cache_v, page_table, lam_params, p['subln_gain'][j],
                                   lam_init=lam_init).reshape(m, qk_width)
        mem_o = _mem_attn(q_mem.reshape(bsz, seq, mem_width), mem_k[l], mem_v[l], tq=min(tm, seq))
        x = _out_proj(mix, mem_o.reshape(m, mem_width), p['w_out'][l], x, tm=tm)
        x = ffn('ffn2', l, x, final_gain=p['norm_final'] if l == depth - 1 else None)
    return x, new_conv, k_out, v_out, ffn_copies


def kernel(x_prompt, x_sample, state_conv, cache_k, cache_v, cache_mem_k, cache_mem_v, page_table, mem_prompt, norm_ffn1, w_ffn1_up, w_ffn1_down, norm_mix, w_in_a, conv_w, w_in_b, lambda_q1, lambda_k1, lambda_q2, lambda_k2, subln_gain, norm_mem, w_mem_kv, w_out, norm_ffn2, w_ffn2_up, w_ffn2_down, norm_kv, w_kv, norm_final):
    bsz_p, s_p, d = x_prompt.shape
    bsz_s, s_s, _ = x_sample.shape
    depth = norm_ffn1.shape[0]
    n_mem = mem_prompt.shape[1]
    mem_width = MEM_HEADS * MEM_HEAD_DIM
    n_pool, page = cache_k.shape[0], cache_k.shape[1]
    k_heads, v_heads = cache_k.shape[2], cache_v.shape[2]
    qk_width = k_heads * cache_k.shape[3]
    past_len = page_table.shape[1] * page
    assert s_s == 1

    def per_layer(w):
        return [_cast_layer(w, l) for l in range(w.shape[0])]

    p = {
        'norm_ffn1': norm_ffn1, 'norm_ffn2': norm_ffn2,
        'norm_mix': norm_mix, 'w_in_a': per_layer(w_in_a), 'conv_w': conv_w, 'w_in_b': per_layer(w_in_b),
        'lambda_q1': lambda_q1, 'lambda_k1': lambda_k1, 'lambda_q2': lambda_q2, 'lambda_k2': lambda_k2,
        'subln_gain': subln_gain, 'w_out': per_layer(w_out),
        'norm_kv': norm_kv, 'w_kv': _cast_layer(w_kv[None], 0), 'norm_final': norm_final,
    }
    w_mem_kv_b = per_layer(w_mem_kv)

    rows_s = bsz_s * s_s
    pos_s = jnp.full((rows_s,), past_len, jnp.int32)
    cmk = cache_mem_k.reshape(depth, bsz_s, n_mem, mem_width)
    cmv = cache_mem_v.reshape(depth, bsz_s, n_mem, mem_width)
    p_sample = dict(p, w_ffn1=[(w_ffn1_up, w_ffn1_down, l) for l in range(depth)],
                    w_ffn2=[(w_ffn2_up, w_ffn2_down, l) for l in range(depth)])
    y_s, conv_s, k_s, v_s, copies = _trunk(
        x_sample.reshape(rows_s, d), p_sample, bsz=bsz_s, seq=s_s,
        tiles=_Tiles(proj=rows_s, ffn=rows_s, ffn_cols=PROMPT_TILES.ffn_cols, attn=0),
        rope=_rope_tables(pos_s),
        mem_k=[cmk[l] for l in range(depth)], mem_v=[cmv[l] for l in range(depth)],
        conv_state=[(state_conv[l, :, 0], state_conv[l, :, 1]) for l in range(depth // 2)],
        cache=(jnp.transpose(cache_k, (0, 2, 3, 1)).reshape(n_pool, qk_width, page),
               jnp.transpose(cache_v, (0, 2, 1, 3)), page_table),
        depth=depth)

    mem_rows = mem_prompt.reshape(bsz_p * n_mem, d)
    mem_kv = [_norm_proj(mem_rows, norm_mem[l], w_mem_kv_b[l],
                         [_Segment(mem_width, (_Out(F32),)), _Segment(mem_width, (_Out(F32),))],
                         tm=n_mem, seq=n_mem)
              for l in range(depth)]
    mem_k_p = [kv[0].reshape(bsz_p, n_mem, mem_width) for kv in mem_kv]
    mem_v_p = [kv[1].reshape(bsz_p, n_mem, mem_width) for kv in mem_kv]
    p_prompt = dict(p, w_ffn1=[copies[('ffn1', l)] for l in range(depth)],
                    w_ffn2=[copies[('ffn2', l)] for l in range(depth)])
    y_p, conv_p, k_t, v_h, _ = _trunk(
        x_prompt.reshape(bsz_p * s_p, d), p_prompt, bsz=bsz_p, seq=s_p, tiles=PROMPT_TILES,
        rope=_rope_tables(jnp.arange(s_p)), mem_k=mem_k_p, mem_v=mem_v_p,
        conv_state=None, cache=None, depth=depth)

    mem_shape = (depth, bsz_p, n_mem, MEM_HEADS, MEM_HEAD_DIM)
    return (y_p.reshape(bsz_p, s_p, d),
            y_s.reshape(bsz_s, s_s, d),
            jnp.stack(conv_p, axis=0),
            jnp.stack(conv_s, axis=0),
            jnp.transpose(k_t.reshape(bsz_p, k_heads, -1, s_p), (0, 3, 1, 2)),
            jnp.transpose(v_h, (0, 2, 1, 3)),
            k_s.reshape(bsz_s, s_s, k_heads, -1),
            v_s.reshape(bsz_s, s_s, v_heads, -1),
            jnp.stack(mem_k_p, axis=0).reshape(mem_shape),
            jnp.stack(mem_v_p, axis=0).reshape(mem_shape))
```

```python
import functools
import math
from typing import NamedTuple

import jax
import jax.numpy as jnp
from jax import lax
from jax.experimental import pallas as pl
from jax.experimental.pallas import tpu as pltpu

F32 = jnp.float32
BF16 = jnp.bfloat16

NORM_EPS = 1e-6
ROPE_THETA = 10000.0
LOG2E = math.log2(math.e)
LANES = 128
BF16_SUBLANES = 16
VMEM_LIMIT = 56 * 1024 * 1024
PROJ_CHUNK = 512
CAST_BLOCK_BYTES = 8 * 1024 * 1024

MEM_HEADS = 4
MEM_HEAD_DIM = 128
DIFF_HEAD_DIM = 64
CONV_WIDTH = 3
ATTN_CHAIN_LANES = 256
DECODE_PAGES_PER_STEP = 8
DECODE_ROWS = 32


class _Tiles(NamedTuple):
    proj: int
    ffn: int
    ffn_cols: int
    attn: int


PROMPT_TILES = _Tiles(proj=512, ffn=512, ffn_cols=512, attn=512)


def _params(*sem):
    return pltpu.CompilerParams(dimension_semantics=sem, vmem_limit_bytes=VMEM_LIMIT)


def _rms(x, g):
    ms = jnp.mean(x * x, axis=-1, keepdims=True)
    return x * lax.rsqrt(ms + NORM_EPS) * g


def _cast_kernel(w_ref, o_ref):
    o_ref[...] = w_ref[...].astype(o_ref.dtype)


def _cast_layer(w3, layer):
    _, k, n = w3.shape
    rows = k
    while rows * n * 4 > CAST_BLOCK_BYTES and rows % 16 == 0:
        rows //= 2
    return pl.pallas_call(
        _cast_kernel,
        grid=(k // rows,),
        in_specs=[pl.BlockSpec((None, rows, n), lambda i: (layer, i, 0))],
        out_specs=pl.BlockSpec((rows, n), lambda i: (i, 0)),
        out_shape=jax.ShapeDtypeStruct((k, n), BF16),
        compiler_params=_params("parallel"),
    )(w3)


def _rope_tile(y, cos, sin_lo, sin_hi):
    outs = []
    for c in range(y.shape[1] // LANES):
        yc = y[:, c * LANES:(c + 1) * LANES]
        up = pltpu.roll(yc, LANES - 32, axis=1)
        dn = pltpu.roll(yc, 32, axis=1)
        outs.append(yc * cos + up * sin_lo + dn * sin_hi)
    return jnp.concatenate(outs, axis=1) if len(outs) > 1 else outs[0]


class _Out(NamedTuple):
    dtype: object
    layout: str = 'rows'


class _Segment(NamedTuple):
    n_cols: int
    outs: tuple
    rope: bool = False
    scale: float = 1.0


def _norm_proj_kernel(*refs, segments, has_rope):
    if has_rope:
        x_ref, g_ref, w_ref, cos_ref, slo_ref, shi_ref = refs[:6]
        out_refs = refs[6:]
    else:
        x_ref, g_ref, w_ref = refs[:3]
        out_refs = refs[3:]
    h = _rms(x_ref[...], g_ref[...]).astype(BF16)
    col, k = 0, 0
    for seg in segments:
        outs = out_refs[k:k + len(seg.outs)]
        for c0 in range(0, seg.n_cols, PROJ_CHUNK):
            c1 = min(c0 + PROJ_CHUNK, seg.n_cols)
            y = jnp.dot(h, w_ref[:, col + c0:col + c1], preferred_element_type=F32)
            if seg.rope:
                y = _rope_tile(y, cos_ref[...], slo_ref[...], shi_ref[...]) * seg.scale
            for o, spec in zip(outs, seg.outs):
                if spec.layout == 'rows':
                    o[:, c0:c1] = y.astype(o.dtype)
                elif spec.layout == 'cols':
                    o[c0:c1, :] = y.T.astype(o.dtype)
                else:
                    for hh in range(c0 // LANES, c1 // LANES):
                        o[hh] = y[:, hh * LANES - c0:(hh + 1) * LANES - c0].astype(o.dtype)
        col += seg.n_cols
        k += len(seg.outs)


def _norm_proj(x, g, w, segments, *, tm, seq, rope=None):
    m, d = x.shape
    assert m % tm == 0 and sum(s.n_cols for s in segments) == w.shape[1]
    has_rope = any(s.rope for s in segments)
    row_major = all(o.layout == 'rows' for s in segments for o in s.outs)
    assert row_major or seq % tm == 0
    per_seq = max(seq // tm, 1)
    in_specs = [
        pl.BlockSpec((tm, d), lambda i: (i, 0)),
        pl.BlockSpec((1, d), lambda i: (0, 0)),
        pl.BlockSpec(w.shape, lambda i: (0, 0), pipeline_mode=pl.Buffered(1)),
    ]
    args = [x, g.reshape(1, d), w]
    if has_rope:
        n_tab = rope[0].shape[0] // tm
        for t in rope:
            in_specs.append(pl.BlockSpec((tm, LANES), lambda i: (i % n_tab, 0)))
            args.append(t)
    out_specs, out_shape = [], []
    for seg in segments:
        n = seg.n_cols
        for o in seg.outs:
            if o.layout == 'rows':
                out_specs.append(pl.BlockSpec((tm, n), lambda i: (i, 0)))
                out_shape.append(jax.ShapeDtypeStruct((m, n), o.dtype))
            elif o.layout == 'cols':
                out_specs.append(pl.BlockSpec((None, n, tm), lambda i: (i // per_seq, 0, i % per_seq)))
                out_shape.append(jax.ShapeDtypeStruct((m // seq, n, seq), o.dtype))
            else:
                out_specs.append(pl.BlockSpec((None, n // LANES, tm, LANES),
                                              lambda i: (i // per_seq, 0, i % per_seq, 0)))
                out_shape.append(jax.ShapeDtypeStruct((m // seq, n // LANES, seq, LANES), o.dtype))
    return pl.pallas_call(
        functools.partial(_norm_proj_kernel, segments=tuple(segments), has_rope=has_rope),
        grid=(m // tm,),
        in_specs=in_specs,
        out_specs=out_specs,
        out_shape=out_shape,
        compiler_params=_params("parallel"),
    )(*args)


def _ffn_kernel(*refs, final_norm, n_side):
    x_ref, g_ref, wg_ref, wu_ref, wd_ref = refs[:5]
    refs = refs[5:]
    if final_norm:
        gf_ref, refs = refs[0], refs[1:]
    side_in, refs = refs[:n_side], refs[n_side:]
    o_ref, refs = refs[0], refs[1:]
    side_out, refs = refs[:n_side], refs[n_side:]
    h_ref, = refs
    j = pl.program_id(1)

    @pl.when(j == 0)
    def _():
        x = x_ref[...]
        h_ref[...] = _rms(x, g_ref[...]).astype(BF16)
        o_ref[...] = x

    for src, dst in zip(side_in, side_out):
        dst[...] = src[...].astype(dst.dtype)

    h = h_ref[...]
    gate = jnp.dot(h, wg_ref[...], preferred_element_type=F32)
    up = jnp.dot(h, wu_ref[...], preferred_element_type=F32)
    act = (gate * jax.nn.sigmoid(gate) * up * 0.5).astype(BF16)
    o_ref[...] += jnp.dot(act, wd_ref[...], preferred_element_type=F32)

    if final_norm:
        @pl.when(j == pl.num_programs(1) - 1)
        def _():
            o_ref[...] = _rms(o_ref[...], gf_ref[...])


def _side_blocks(n_rows, n_steps):
    for nb in range(n_steps, 0, -1):
        if n_rows % nb == 0 and (n_rows // nb) % BF16_SUBLANES == 0:
            return nb
    raise ValueError(f"no aligned row split of {n_rows} rows over {n_steps} steps")


def _ffn(x, g, w_up, w_down, *, tm, tf, final_gain=None, side=()):
    m, d = x.shape
    f = w_down.shape[0]
    nf = f // tf
    assert m % tm == 0 and f % tf == 0 and w_up.shape[1] == 2 * f
    n_steps = (m // tm) * nf
    in_specs = [
        pl.BlockSpec((tm, d), lambda i, j: (i, 0), pipeline_mode=pl.Buffered(1)),
        pl.BlockSpec((1, d), lambda i, j: (0, 0)),
        pl.BlockSpec((d, tf), lambda i, j: (0, j)),
        pl.BlockSpec((d, tf), lambda i, j: (0, nf + j)),
        pl.BlockSpec((tf, d), lambda i, j: (j, 0)),
    ]
    args = [x, g.reshape(1, d), w_up, w_up, w_down]
    if final_gain is not None:
        in_specs.append(pl.BlockSpec((1, d), lambda i, j: (0, 0)))
        args.append(final_gain.reshape(1, d))
    out_specs = [pl.BlockSpec((tm, d), lambda i, j: (i, 0))]
    out_shape = [jax.ShapeDtypeStruct((m, d), F32)]
    for w3, layer in side:
        _, rows, cols = w3.shape
        nb = _side_blocks(rows, n_steps)
        rb = rows // nb
        in_specs.append(pl.BlockSpec((None, rb, cols),
                                     lambda i, j, nb=nb, layer=layer: (layer, jnp.minimum(i * nf + j, nb - 1), 0)))
        out_specs.append(pl.BlockSpec((rb, cols), lambda i, j, nb=nb: (jnp.minimum(i * nf + j, nb - 1), 0)))
        out_shape.append(jax.ShapeDtypeStruct((rows, cols), BF16))
        args.append(w3)
    outs = pl.pallas_call(
        functools.partial(_ffn_kernel, final_norm=final_gain is not None, n_side=len(side)),
        grid=(m // tm, nf),
        in_specs=in_specs,
        out_specs=out_specs,
        out_shape=out_shape,
        scratch_shapes=[pltpu.VMEM((tm, d), BF16)],
        compiler_params=_params("arbitrary", "arbitrary"),
    )(*args)
    return outs[0], list(outs[1:])


def _conv_mixer_kernel(x_ref, g_ref, w_ref, cw_ref, mix_ref, qmem_ref, st_ref, carry_ref, ubuf,
                       *, tiles_per_seq):
    tm = x_ref.shape[0]
    cdim = mix_ref.shape[1]
    chunk = ubuf.shape[1]
    i = pl.program_id(0)

    @pl.when(i % tiles_per_seq == 0)
    def _():
        carry_ref[...] = jnp.zeros(carry_ref.shape, F32)

    h = _rms(x_ref[...], g_ref[...]).astype(BF16)
    cw = cw_ref[...]
    for c0 in range(0, cdim, chunk):
        cs = slice(c0, c0 + chunk)
        b_gate = jnp.dot(h, w_ref[:, c0:c0 + chunk], preferred_element_type=F32)
        c_gate = jnp.dot(h, w_ref[:, cdim + c0:cdim + c0 + chunk], preferred_element_type=F32)
        x_in = jnp.dot(h, w_ref[:, 2 * cdim + c0:2 * cdim + c0 + chunk], preferred_element_type=F32)
        u = c_gate * x_in
        ubuf[0:8, :] = carry_ref[:, cs]
        ubuf[8:tm + 8, :] = u
        conv = cw[0:1, cs] * ubuf[6:tm + 6, :] + cw[1:2, cs] * ubuf[7:tm + 7, :] + cw[2:3, cs] * u
        mix_ref[:, cs] = (b_gate * conv).astype(mix_ref.dtype)
        st_ref[:, cs] = u[tm - (CONV_WIDTH - 1):tm]
        carry_ref[:, cs] = u[tm - 8:tm]
    qmem_ref[...] = jnp.dot(h, w_ref[:, 3 * cdim:], preferred_element_type=F32).astype(qmem_ref.dtype)


def _conv_mixer_prompt(x, g, w, conv_w, *, bsz, tm, cdim):
    m, d = x.shape
    tiles_per_seq = m // bsz // tm
    n_mem_q = w.shape[1] - 3 * cdim
    return pl.pallas_call(
        functools.partial(_conv_mixer_kernel, tiles_per_seq=tiles_per_seq),
        grid=(m // tm,),
        in_specs=[pl.BlockSpec((tm, d), lambda i: (i, 0)),
                  pl.BlockSpec((1, d), lambda i: (0, 0)),
                  pl.BlockSpec(w.shape, lambda i: (0, 0), pipeline_mode=pl.Buffered(1)),
                  pl.BlockSpec((CONV_WIDTH, cdim), lambda i: (0, 0))],
        out_specs=[pl.BlockSpec((tm, cdim), lambda i: (i, 0)),
                   pl.BlockSpec((tm, n_mem_q), lambda i: (i, 0)),
                   pl.BlockSpec((None, CONV_WIDTH - 1, cdim), lambda i: (i // tiles_per_seq, 0, 0))],
        out_shape=[jax.ShapeDtypeStruct((m, cdim), BF16),
                   jax.ShapeDtypeStruct((m, n_mem_q), BF16),
                   jax.ShapeDtypeStruct((bsz, CONV_WIDTH - 1, cdim), F32)],
        scratch_shapes=[pltpu.VMEM((8, cdim), F32), pltpu.VMEM((tm + 8, PROJ_CHUNK), F32)],
        compiler_params=_params("arbitrary"),
    )(x, g.reshape(1, d), w, conv_w)


def _conv_gate_step_kernel(b_ref, c_ref, xin_ref, s0_ref, s1_ref, w_ref, mix_ref, u_ref):
    u = c_ref[...].astype(F32) * xin_ref[...].astype(F32)
    w = w_ref[...]
    conv = w[0:1] * s0_ref[...] + w[1:2] * s1_ref[...] + w[2:3] * u
    mix_ref[...] = (b_ref[...].astype(F32) * conv).astype(mix_ref.dtype)
    u_ref[...] = u


def _conv_gate_step(proj, s0, s1, conv_w, *, cdim):
    m = proj.shape[0]
    col = lambda cb: pl.BlockSpec((m, cdim), lambda i: (0, cb))
    full = lambda r: pl.BlockSpec((r, cdim), lambda i: (0, 0))
    return pl.pallas_call(
        _conv_gate_step_kernel,
        grid=(1,),
        in_specs=[col(0), col(1), col(2), full(m), full(m), full(CONV_WIDTH)],
        out_specs=[full(m), full(m)],
        out_shape=[jax.ShapeDtypeStruct((m, cdim), BF16), jax.ShapeDtypeStruct((m, cdim), F32)],
        compiler_params=_params("arbitrary"),
    )(proj, proj, proj, s0, s1, conv_w)


def _mem_attn_kernel(q_ref, k_ref, v_ref, o_ref):
    rows = q_ref.shape[0]
    q = q_ref[...].astype(BF16)
    if rows < 8:
        q = jnp.broadcast_to(q[0:1], (8, q.shape[1]))
    k = k_ref[...].astype(BF16)
    v = v_ref[...].astype(BF16)
    scale = MEM_HEAD_DIM ** -0.5
    for h in range(MEM_HEADS):
        hs = slice(h * MEM_HEAD_DIM, (h + 1) * MEM_HEAD_DIM)
        s = lax.dot_general(q[:, hs], k[:, hs], (((1,), (1,)), ((), ())),
                            preferred_element_type=F32) * scale
        p = jnp.exp(s - jnp.max(s, axis=-1, keepdims=True))
        l = jnp.sum(p, axis=-1, keepdims=True)
        o = jnp.dot(p.astype(BF16), v[:, hs], preferred_element_type=F32) / l
        o_ref[:, hs] = o[:rows].astype(o_ref.dtype)


def _mem_attn(q3, k3, v3, *, tq):
    nb, rows, width = q3.shape
    n_mem = k3.shape[1]
    return pl.pallas_call(
        _mem_attn_kernel,
        grid=(nb, rows // tq),
        in_specs=[pl.BlockSpec((None, tq, width), lambda b, i: (b, i, 0)),
                  pl.BlockSpec((None, n_mem, width), lambda b, i: (b, 0, 0)),
                  pl.BlockSpec((None, n_mem, width), lambda b, i: (b, 0, 0))],
        out_specs=pl.BlockSpec((None, tq, width), lambda b, i: (b, i, 0)),
        out_shape=jax.ShapeDtypeStruct((nb, rows, width), BF16),
        compiler_params=_params("parallel", "parallel"),
    )(q3, k3, v3)


def _out_proj_kernel(a1_ref, a2_ref, w_ref, x_ref, o_ref):
    n1 = a1_ref.shape[1]
    acc = jnp.dot(a1_ref[...], w_ref[0:n1, :], preferred_element_type=F32)
    acc = acc + jnp.dot(a2_ref[...], w_ref[n1:, :], preferred_element_type=F32)
    o_ref[...] = x_ref[...] + acc


def _out_proj(a1, a2, w, x, *, tm):
    m, d = x.shape
    n1, n2 = a1.shape[1], a2.shape[1]
    return pl.pallas_call(
        _out_proj_kernel,
        grid=(m // tm,),
        in_specs=[pl.BlockSpec((tm, n1), lambda i: (i, 0)),
                  pl.BlockSpec((tm, n2), lambda i: (i, 0)),
                  pl.BlockSpec((n1 + n2, d), lambda i: (0, 0), pipeline_mode=pl.Buffered(1)),
                  pl.BlockSpec((tm, d), lambda i: (i, 0))],
        out_specs=pl.BlockSpec((tm, d), lambda i: (i, 0)),
        out_shape=jax.ShapeDtypeStruct((m, d), F32),
        compiler_params=_params("parallel"),
    )(a1, a2, w, x)


def _lambda_value(lq1_ref, lk1_ref, lq2_ref, lk2_ref, lam_init):
    a = jnp.exp(jnp.sum(lq1_ref[...] * lk1_ref[...], axis=-1, keepdims=True))
    b = jnp.exp(jnp.sum(lq2_ref[...] * lk2_ref[...], axis=-1, keepdims=True))
    return a - b + lam_init


def _subln(d, gain, lam_init):
    return _rms(d, gain) * (1.0 - lam_init)


def _diff_attn_kernel(q_ref, k_ref, v_ref, lq1_ref, lk1_ref, lq2_ref, lk2_ref, g_ref, o_ref,
                      q2_ref, vt_ref, m_ref, l_ref, acc_ref, *, lam_init):
    tq = q_ref.shape[0]
    tk = vt_ref.shape[2]
    qi = pl.program_id(2)

    @pl.when(qi == 0)
    def _():
        for c in range(vt_ref.shape[0]):
            vt_ref[c] = v_ref[c * tk:(c + 1) * tk, :].T

    q = q_ref[...]
    lane = lax.broadcasted_iota(jnp.int32, q.shape, 1)
    zero = jnp.zeros_like(q)
    q2_ref[0:tq, :] = jnp.where(lane < DIFF_HEAD_DIM, q, zero)
    q2_ref[tq:2 * tq, :] = jnp.where(lane < DIFF_HEAD_DIM, zero, q)
    m_ref[...] = jnp.full(m_ref.shape, -jnp.inf, F32)
    l_ref[...] = jnp.zeros(l_ref.shape, F32)
    acc_ref[...] = jnp.zeros(acc_ref.shape, F32)

    chains = range(0, 2 * tq, ATTN_CHAIN_LANES)

    def n_keys(c, masked):
        return (c % tq) + ATTN_CHAIN_LANES if masked else tk

    def scores(kj, masked):
        start = pl.multiple_of(kj * tk, tk)
        k = k_ref[pl.ds(start, tk), :]
        return [lax.dot_general(k[0:n_keys(c, masked)], q2_ref[c:c + ATTN_CHAIN_LANES, :],
                                (((1,), (1,)), ((), ())), preferred_element_type=F32) for c in chains]

    def absorb(kj, sts, masked):
        vt = vt_ref[kj]
        for c, st in zip(chains, sts):
            cs = slice(c, c + ATTN_CHAIN_LANES)
            if masked:
                key = lax.broadcasted_iota(jnp.int32, st.shape, 0)
                qry = lax.broadcasted_iota(jnp.int32, st.shape, 1) + (c % tq)
                st = jnp.where(key <= qry, st, -jnp.inf)
            m_prev = m_ref[:, cs]
            m_next = jnp.maximum(m_prev, jnp.max(st, axis=0, keepdims=True))
            alpha = jnp.exp2(m_prev - m_next)
            p = jnp.exp2(st - m_next)
            l_ref[:, cs] = alpha * l_ref[:, cs] + jnp.sum(p, axis=0, keepdims=True)
            acc_ref[:, cs] = alpha * acc_ref[:, cs] + jnp.dot(vt[:, 0:n_keys(c, masked)], p.astype(BF16),
                                                              preferred_element_type=F32)
            m_ref[:, cs] = m_next

    def pair(t, carry):
        sa, sb = scores(2 * t, False), scores(2 * t + 1, False)
        absorb(2 * t, sa, False)
        absorb(2 * t + 1, sb, False)
        return carry

    lax.fori_loop(0, qi // 2, pair, 0)

    @pl.when(qi % 2 == 0)
    def _():
        absorb(qi, scores(qi, True), True)

    @pl.when(qi % 2 == 1)
    def _():
        sa, sd = scores(qi - 1, False), scores(qi, True)
        absorb(qi - 1, sa, False)
        absorb(qi, sd, True)

    ot = acc_ref[...] / l_ref[...]
    lam = _lambda_value(lq1_ref, lk1_ref, lq2_ref, lk2_ref, lam_init)
    dt = ot[:, 0:tq] - lam * ot[:, tq:2 * tq]
    yt = dt * lax.rsqrt(jnp.mean(dt * dt, axis=0, keepdims=True) + NORM_EPS)
    o_ref[...] = (yt.T * g_ref[...] * (1.0 - lam_init)).astype(o_ref.dtype)


def _diff_attn_prompt(q, k, v4, lam_params, gain, *, bsz, seq, tq, lam_init):
    m = bsz * seq
    width = k.shape[1]
    heads = width // LANES
    nq = seq // tq
    small = lambda a: pl.BlockSpec(a.shape, lambda b, h, i: (0, 0))
    lam_args = [a.reshape(1, -1) for a in lam_params]
    gain = gain.reshape(1, -1)
    return pl.pallas_call(
        functools.partial(_diff_attn_kernel, lam_init=lam_init),
        grid=(bsz, heads, nq),
        in_specs=[pl.BlockSpec((tq, LANES), lambda b, h, i: (b * nq + i, h)),
                  pl.BlockSpec((seq, LANES), lambda b, h, i: (b, h)),
                  pl.BlockSpec((None, None, seq, LANES), lambda b, h, i: (b, h, 0, 0))]
                 + [small(a) for a in lam_args] + [small(gain)],
        out_specs=pl.BlockSpec((tq, LANES), lambda b, h, i: (b * nq + i, h)),
        out_shape=jax.ShapeDtypeStruct((m, width), BF16),
        scratch_shapes=[pltpu.VMEM((2 * tq, LANES), BF16),
                        pltpu.VMEM((seq // tq, LANES, tq), BF16),
                        pltpu.VMEM((1, 2 * tq), F32),
                        pltpu.VMEM((1, 2 * tq), F32),
                        pltpu.VMEM((LANES, 2 * tq), F32)],
        compiler_params=_params("parallel", "parallel", "arbitrary"),
    )(q, k, v4, *lam_args, gain)


def _decode_attn_kernel(pt_ref, q_ref, kn_ref, vn_ref, *refs, n_pages, lam_init):
    del pt_ref
    k_refs, v_refs = refs[:n_pages], refs[n_pages:2 * n_pages]
    lq1_ref, lk1_ref, lq2_ref, lk2_ref, g_ref, o_ref, qbd_ref, m_ref, l_ref, acc_ref = refs[2 * n_pages:]
    g = pl.program_id(1)
    width = q_ref.shape[1]

    @pl.when(g == 0)
    def _():
        row = lax.broadcasted_iota(jnp.int32, (DECODE_ROWS, width), 0)
        lane = lax.broadcasted_iota(jnp.int32, (DECODE_ROWS, width), 1)
        qb = jnp.broadcast_to(q_ref[...], (DECODE_ROWS, width))
        keep = (lane >= row * DIFF_HEAD_DIM) & (lane < (row + 1) * DIFF_HEAD_DIM)
        qbd_ref[...] = jnp.where(keep, qb, 0.0).astype(BF16)
        m_ref[...] = jnp.full(m_ref.shape, -jnp.inf, F32)
        l_ref[...] = jnp.zeros(l_ref.shape, F32)
        acc_ref[...] = jnp.zeros(acc_ref.shape, F32)

    qbd = qbd_ref[...]
    s = jnp.concatenate(
        [jnp.dot(qbd, kr[...].astype(BF16), preferred_element_type=F32) for kr in k_refs], axis=1)
    m_prev = m_ref[:, 0:1]
    m_next = jnp.maximum(m_prev, jnp.max(s, axis=-1, keepdims=True))
    alpha = jnp.exp(m_prev - m_next)
    p = jnp.exp(s - m_next)
    l_next = alpha * l_ref[:, 0:1] + jnp.sum(p, axis=-1, keepdims=True)
    pb = p.astype(BF16)
    page = k_refs[0].shape[1]
    for h in range(width // LANES):
        hs = slice(h * LANES, (h + 1) * LANES)
        pv = jnp.dot(pb[:, 0:page], v_refs[0][h].astype(BF16), preferred_element_type=F32)
        for r in range(1, n_pages):
            pv = pv + jnp.dot(pb[:, r * page:(r + 1) * page], v_refs[r][h].astype(BF16),
                              preferred_element_type=F32)
        acc_ref[:, hs] = alpha * acc_ref[:, hs] + pv
    m_ref[...] = jnp.broadcast_to(m_next, m_ref.shape)
    l_ref[...] = jnp.broadcast_to(l_next, l_ref.shape)

    @pl.when(g == pl.num_programs(1) - 1)
    def _():
        kn = kn_ref[...].astype(BF16).astype(F32)
        s_new = jnp.sum(qbd_ref[...].astype(F32) * kn, axis=-1, keepdims=True)
        m_old = m_ref[:, 0:1]
        m_fin = jnp.maximum(m_old, s_new)
        a = jnp.exp(m_old - m_fin)
        p_new = jnp.exp(s_new - m_fin)
        l_fin = a * l_ref[:, 0:1] + p_new
        o = (a * acc_ref[...] + p_new * vn_ref[...]) / l_fin
        lam = _lambda_value(lq1_ref, lk1_ref, lq2_ref, lk2_ref, lam_init)
        gain = g_ref[...]
        for h in range(width // LANES):
            hs = slice(h * LANES, (h + 1) * LANES)
            d = o[2 * h:2 * h + 1, hs] - lam * o[2 * h + 1:2 * h + 2, hs]
            o_ref[:, hs] = _subln(d, gain, lam_init).astype(o_ref.dtype)


def _decode_attn(q, k_new, v_new, cache_k, cache_v, page_table, lam_params, gain, *, lam_init):
    nb, _, width = q.shape
    page = cache_k.shape[2]
    n_used = page_table.shape[1]
    npg = DECODE_PAGES_PER_STEP
    assert n_used % npg == 0
    row = pl.BlockSpec((None, 1, width), lambda b, g, pt: (b, 0, 0))
    small = lambda a: pl.BlockSpec(a.shape, lambda b, g, pt: (0, 0))
    k_spec = lambda r: pl.BlockSpec((None, width, page),
                                    lambda b, g, pt: (pt[b, g * npg + r], 0, 0))
    v_spec = lambda r: pl.BlockSpec((None, width // LANES, page, LANES),
                                    lambda b, g, pt: (pt[b, g * npg + r], 0, 0, 0))
    lam_args = [a.reshape(1, -1) for a in lam_params]
    gain = gain.reshape(1, -1)
    grid_spec = pltpu.PrefetchScalarGridSpec(
        num_scalar_prefetch=1,
        grid=(nb, n_used // npg),
        in_specs=[row, row, row] + [k_spec(r) for r in range(npg)] + [v_spec(r) for r in range(npg)]
                 + [small(a) for a in lam_args] + [small(gain)],
        out_specs=pl.BlockSpec((None, 1, width), lambda b, g, pt: (b, 0, 0)),
        scratch_shapes=[pltpu.VMEM((DECODE_ROWS, width), BF16),
                        pltpu.VMEM((DECODE_ROWS, LANES), F32),
                        pltpu.VMEM((DECODE_ROWS, LANES), F32),
                        pltpu.VMEM((DECODE_ROWS, width), F32)],
    )
    return pl.pallas_call(
        functools.partial(_decode_attn_kernel, n_pages=npg, lam_init=lam_init),
        grid_spec=grid_spec,
        out_shape=jax.ShapeDtypeStruct((nb, 1, width), BF16),
        compiler_params=_params("parallel", "arbitrary"),
    )(page_table, q, k_new, v_new, *([cache_k] * npg), *([cache_v] * npg), *lam_args, gain)


def _rope_tables(pos):
    half = DIFF_HEAD_DIM // 2
    inv_freq = ROPE_THETA ** (-jnp.arange(half, dtype=F32) / half)
    ang = pos.astype(F32)[:, None] * inv_freq[None, :]
    cos, sin, zero = jnp.cos(ang), jnp.sin(ang), jnp.zeros_like(ang)
    reps = LANES // DIFF_HEAD_DIM
    cos_t = jnp.tile(jnp.concatenate([cos, cos], axis=1), (1, reps))
    sin_lo = jnp.tile(jnp.concatenate([-sin, zero], axis=1), (1, reps))
    sin_hi = jnp.tile(jnp.concatenate([zero, sin], axis=1), (1, reps))
    return cos_t, sin_lo, sin_hi


def _trunk(x, p, wb, side_plan, *, bsz, seq, tiles, rope, mem_k, mem_v, conv_state, cache, depth):
    m, d = x.shape
    n_a = depth // 2
    prompt = cache is None
    cdim = p['conv_w'].shape[-1]
    qk_width = p['qk_width']
    mem_width = MEM_HEADS * MEM_HEAD_DIM
    tm = tiles.proj
    act = BF16 if prompt else F32
    new_conv = []
    k_out = v_out = k_b = v_b = None

    def ffn(name, l, x, final_gain=None):
        jobs = side_plan.get((name, l), [])
        y, copies = _ffn(x, p['norm_' + name][l], wb[('w_' + name + '_up', l)], wb[('w_' + name + '_down', l)],
                         tm=tiles.ffn, tf=tiles.ffn_cols, final_gain=final_gain,
                         side=[(w3, layer) for _, w3, layer in jobs])
        for (key, _, _), copy in zip(jobs, copies):
            wb[key] = copy
        return y

    for l in range(depth):
        if l == n_a:
            if prompt:
                k_out, k_b, v_out, v_b = _norm_proj(
                    x, p['norm_kv'], wb[('w_kv', 0)],
                    [_Segment(qk_width, (_Out(F32, 'cols'), _Out(BF16)), rope=True),
                     _Segment(qk_width, (_Out(F32, 'heads'), _Out(BF16, 'heads')))],
                    tm=tm, seq=seq, rope=rope)
            else:
                k_out, v_out = _norm_proj(
                    x, p['norm_kv'], wb[('w_kv', 0)],
                    [_Segment(qk_width, (_Out(F32),), rope=True), _Segment(qk_width, (_Out(F32),))],
                    tm=tm, seq=seq, rope=rope)
        x = ffn('ffn1', l, x)
        if l < n_a:
            if prompt:
                mix, q_mem, st = _conv_mixer_prompt(x, p['norm_mix'][l], wb[('w_in_a', l)], p['conv_w'][l],
                                                    bsz=bsz, tm=tm, cdim=cdim)
            else:
                proj, q_mem = _norm_proj(x, p['norm_mix'][l], wb[('w_in_a', l)],
                                         [_Segment(3 * cdim, (_Out(act),)), _Segment(mem_width, (_Out(act),))],
                                         tm=tm, seq=seq)
                s0, s1 = conv_state[l]
                mix, u = _conv_gate_step(proj, s0, s1, p['conv_w'][l], cdim=cdim)
                st = jnp.stack([s1, u], axis=1)
            new_conv.append(st)
        else:
            j = l - n_a
            lam_init = 0.8 - 0.6 * math.exp(-0.3 * l)
            lam_params = (p['lambda_q1'][j], p['lambda_k1'][j], p['lambda_q2'][j], p['lambda_k2'][j])
            q_scale = DIFF_HEAD_DIM ** -0.5 * (LOG2E if prompt else 1.0)
            q, q_mem = _norm_proj(
                x, p['norm_mix'][l], wb[('w_in_b', j)],
                [_Segment(qk_width, (_Out(act),), rope=True, scale=q_scale),
                 _Segment(mem_width, (_Out(act),))],
                tm=tm, seq=seq, rope=rope)
            if prompt:
                mix = _diff_attn_prompt(q, k_b, v_b, lam_params, p['subln_gain'][j],
                                        bsz=bsz, seq=seq, tq=tiles.attn, lam_init=lam_init)
            else:
                cache_k, cache_v, page_table = cache
                mix = _decode_attn(q.reshape(bsz, 1, qk_width),
                                   k_out.reshape(bsz, 1, qk_width), v_out.reshape(bsz, 1, qk_width),
                                   cache_k, cache_v, page_table, lam_params, p['subln_gain'][j],
                                   lam_init=lam_init).reshape(m, qk_width)
        mem_o = _mem_attn(q_mem.reshape(bsz, seq, mem_width), mem_k[l], mem_v[l], tq=min(tm, seq))
        x = _out_proj(mix, mem_o.reshape(m, mem_width), wb[('w_out', l)], x, tm=tm)
        x = ffn('ffn2', l, x, final_gain=p['norm_final'] if l == depth - 1 else None)
    return x, new_conv, k_out, v_out


def kernel(x_prompt, x_sample, state_conv, cache_k, cache_v, cache_mem_k, cache_mem_v, page_table, mem_prompt, norm_ffn1, w_ffn1_up, w_ffn1_down, norm_mix, w_in_a, conv_w, w_in_b, lambda_q1, lambda_k1, lambda_q2, lambda_k2, subln_gain, norm_mem, w_mem_kv, w_out, norm_ffn2, w_ffn2_up, w_ffn2_down, norm_kv, w_kv, norm_final):
    bsz_p, s_p, d = x_prompt.shape
    bsz_s, s_s, _ = x_sample.shape
    depth = norm_ffn1.shape[0]
    n_mem = mem_prompt.shape[1]
    mem_width = MEM_HEADS * MEM_HEAD_DIM
    n_pool, page = cache_k.shape[0], cache_k.shape[1]
    k_heads, v_heads = cache_k.shape[2], cache_v.shape[2]
    qk_width = k_heads * cache_k.shape[3]
    past_len = page_table.shape[1] * page
    assert s_s == 1

    n_a = depth // 2
    p = {
        'norm_ffn1': norm_ffn1, 'norm_ffn2': norm_ffn2, 'norm_mix': norm_mix, 'conv_w': conv_w,
        'lambda_q1': lambda_q1, 'lambda_k1': lambda_k1, 'lambda_q2': lambda_q2, 'lambda_k2': lambda_k2,
        'subln_gain': subln_gain, 'norm_kv': norm_kv, 'norm_final': norm_final, 'qk_width': qk_width,
    }
    stacked = {'w_ffn1_up': w_ffn1_up, 'w_ffn1_down': w_ffn1_down, 'w_ffn2_up': w_ffn2_up,
               'w_ffn2_down': w_ffn2_down, 'w_in_a': w_in_a, 'w_in_b': w_in_b, 'w_out': w_out,
               'w_kv': w_kv[None]}

    def job(name, layer):
        return ((name, layer), stacked[name], layer)

    first = [job('w_ffn1_up', 0), job('w_ffn1_down', 0)] + ([job('w_kv', 0)] if n_a == 0 else [])
    wb = {key: _cast_layer(w3, layer) for key, w3, layer in first}
    side_plan = {}
    for l in range(depth):
        mixer = job('w_in_a', l) if l < n_a else job('w_in_b', l - n_a)
        side_plan[('ffn1', l)] = [mixer, job('w_out', l), job('w_ffn2_up', l), job('w_ffn2_down', l)]
        if l + 1 < depth:
            shared_kv = [job('w_kv', 0)] if l + 1 == n_a else []
            side_plan[('ffn2', l)] = shared_kv + [job('w_ffn1_up', l + 1), job('w_ffn1_down', l + 1)]

    mem_rows = mem_prompt.reshape(bsz_p * n_mem, d)
    mem_kv = [_norm_proj(mem_rows, norm_mem[l], _cast_layer(w_mem_kv, l),
                         [_Segment(mem_width, (_Out(F32),)), _Segment(mem_width, (_Out(F32),))],
                         tm=n_mem, seq=n_mem)
              for l in range(depth)]
    mem_k_p = [kv[0].reshape(bsz_p, n_mem, mem_width) for kv in mem_kv]
    mem_v_p = [kv[1].reshape(bsz_p, n_mem, mem_width) for kv in mem_kv]
    y_p, conv_p, k_t, v_h = _trunk(
        x_prompt.reshape(bsz_p * s_p, d), p, wb, side_plan, bsz=bsz_p, seq=s_p, tiles=PROMPT_TILES,
        rope=_rope_tables(jnp.arange(s_p)), mem_k=mem_k_p, mem_v=mem_v_p,
        conv_state=None, cache=None, depth=depth)

    rows_s = bsz_s * s_s
    pos_s = jnp.full((rows_s,), past_len, jnp.int32)
    cmk = cache_mem_k.reshape(depth, bsz_s, n_mem, mem_width)
    cmv = cache_mem_v.reshape(depth, bsz_s, n_mem, mem_width)
    y_s, conv_s, k_s, v_s = _trunk(
        x_sample.reshape(rows_s, d), p, wb, {}, bsz=bsz_s, seq=s_s,
        tiles=_Tiles(proj=rows_s, ffn=rows_s, ffn_cols=PROMPT_TILES.ffn_cols, attn=0),
        rope=_rope_tables(pos_s),
        mem_k=[cmk[l] for l in range(depth)], mem_v=[cmv[l] for l in range(depth)],
        conv_state=[(state_conv[l, :, 0], state_conv[l, :, 1]) for l in range(n_a)],
        cache=(jnp.transpose(cache_k, (0, 2, 3, 1)).reshape(n_pool, qk_width, page),
               jnp.transpose(cache_v, (0, 2, 1, 3)), page_table),
        depth=depth)

    mem_shape = (depth, bsz_p, n_mem, MEM_HEADS, MEM_HEAD_DIM)
    return (y_p.reshape(bsz_p, s_p, d),
            y_s.reshape(bsz_s, s_s, d),
            jnp.stack(conv_p, axis=0),
            jnp.stack(conv_s, axis=0),
            jnp.transpose(k_t.reshape(bsz_p, k_heads, -1, s_p), (0, 3, 1, 2)),
            jnp.transpose(v_h, (0, 2, 1, 3)),
            k_s.reshape(bsz_s, s_s, k_heads, -1),
            v_s.reshape(bsz_s, s_s, v_heads, -1),
            jnp.stack(mem_k_p, axis=0).reshape(mem_shape),
            jnp.stack(mem_v_p, axis=0).reshape(mem_shape))
```

```python
import functools
import math
from typing import NamedTuple

import jax
import jax.numpy as jnp
from jax import lax
from jax.experimental import pallas as pl
from jax.experimental.pallas import tpu as pltpu

F32 = jnp.float32
BF16 = jnp.bfloat16

NORM_EPS = 1e-6
ROPE_THETA = 10000.0
LOG2E = math.log2(math.e)
LANES = 128
BF16_SUBLANES = 16
VMEM_LIMIT = 56 * 1024 * 1024
PROJ_CHUNK = 512
CAST_BLOCK_BYTES = 8 * 1024 * 1024

MEM_HEADS = 4
MEM_HEAD_DIM = 128
DIFF_HEAD_DIM = 64
CONV_WIDTH = 3
ATTN_CHAIN_LANES = 256
DECODE_PAGES_PER_STEP = 8
DECODE_ROWS = 32


class _Tiles(NamedTuple):
    proj: int
    ffn: int
    ffn_cols: int
    attn: int


PROMPT_TILES = _Tiles(proj=512, ffn=512, ffn_cols=512, attn=512)


def _params(*sem):
    return pltpu.CompilerParams(dimension_semantics=sem, vmem_limit_bytes=VMEM_LIMIT)


def _rms(x, g):
    ms = jnp.mean(x * x, axis=-1, keepdims=True)
    return x * lax.rsqrt(ms + NORM_EPS) * g


def _cast_kernel(w_ref, o_ref):
    o_ref[...] = w_ref[...].astype(o_ref.dtype)


def _cast_layer(w3, layer):
    _, k, n = w3.shape
    rows = k
    while rows * n * 4 > CAST_BLOCK_BYTES and rows % 16 == 0:
        rows //= 2
    return pl.pallas_call(
        _cast_kernel,
        grid=(k // rows,),
        in_specs=[pl.BlockSpec((None, rows, n), lambda i: (layer, i, 0))],
        out_specs=pl.BlockSpec((rows, n), lambda i: (i, 0)),
        out_shape=jax.ShapeDtypeStruct((k, n), BF16),
        compiler_params=_params("parallel"),
    )(w3)


def _rope_tile(y, cos, sin_lo, sin_hi):
    outs = []
    for c in range(y.shape[1] // LANES):
        yc = y[:, c * LANES:(c + 1) * LANES]
        up = pltpu.roll(yc, LANES - 32, axis=1)
        dn = pltpu.roll(yc, 32, axis=1)
        outs.append(yc * cos + up * sin_lo + dn * sin_hi)
    return jnp.concatenate(outs, axis=1) if len(outs) > 1 else outs[0]


class _Out(NamedTuple):
    dtype: object
    layout: str = 'rows'


class _Segment(NamedTuple):
    n_cols: int
    outs: tuple
    rope: bool = False
    scale: float = 1.0


def _norm_proj_kernel(*refs, segments, has_rope):
    if has_rope:
        x_ref, g_ref, w_ref, cos_ref, slo_ref, shi_ref = refs[:6]
        out_refs = refs[6:]
    else:
        x_ref, g_ref, w_ref = refs[:3]
        out_refs = refs[3:]
    h = _rms(x_ref[...], g_ref[...]).astype(BF16)
    col, k = 0, 0
    for seg in segments:
        outs = out_refs[k:k + len(seg.outs)]
        for c0 in range(0, seg.n_cols, PROJ_CHUNK):
            c1 = min(c0 + PROJ_CHUNK, seg.n_cols)
            y = jnp.dot(h, w_ref[:, col + c0:col + c1], preferred_element_type=F32)
            if seg.rope:
                y = _rope_tile(y, cos_ref[...], slo_ref[...], shi_ref[...]) * seg.scale
            for o, spec in zip(outs, seg.outs):
                if spec.layout == 'rows':
                    o[:, c0:c1] = y.astype(o.dtype)
                elif spec.layout == 'cols':
                    o[c0:c1, :] = y.T.astype(o.dtype)
                else:
                    for hh in range(c0 // LANES, c1 // LANES):
                        o[hh] = y[:, hh * LANES - c0:(hh + 1) * LANES - c0].astype(o.dtype)
        col += seg.n_cols
        k += len(seg.outs)


def _norm_proj(x, g, w, segments, *, tm, seq, rope=None):
    m, d = x.shape
    assert m % tm == 0 and sum(s.n_cols for s in segments) == w.shape[1]
    has_rope = any(s.rope for s in segments)
    row_major = all(o.layout == 'rows' for s in segments for o in s.outs)
    assert row_major or seq % tm == 0
    per_seq = max(seq // tm, 1)
    in_specs = [
        pl.BlockSpec((tm, d), lambda i: (i, 0)),
        pl.BlockSpec((1, d), lambda i: (0, 0)),
        pl.BlockSpec(w.shape, lambda i: (0, 0), pipeline_mode=pl.Buffered(1)),
    ]
    args = [x, g.reshape(1, d), w]
    if has_rope:
        n_tab = rope[0].shape[0] // tm
        for t in rope:
            in_specs.append(pl.BlockSpec((tm, LANES), lambda i: (i % n_tab, 0)))
            args.append(t)
    out_specs, out_shape = [], []
    for seg in segments:
        n = seg.n_cols
        for o in seg.outs:
            if o.layout == 'rows':
                out_specs.append(pl.BlockSpec((tm, n), lambda i: (i, 0)))
                out_shape.append(jax.ShapeDtypeStruct((m, n), o.dtype))
            elif o.layout == 'cols':
                out_specs.append(pl.BlockSpec((None, n, tm), lambda i: (i // per_seq, 0, i % per_seq)))
                out_shape.append(jax.ShapeDtypeStruct((m // seq, n, seq), o.dtype))
            else:
                out_specs.append(pl.BlockSpec((None, n // LANES, tm, LANES),
                                              lambda i: (i // per_seq, 0, i % per_seq, 0)))
                out_shape.append(jax.ShapeDtypeStruct((m // seq, n // LANES, seq, LANES), o.dtype))
    return pl.pallas_call(
        functools.partial(_norm_proj_kernel, segments=tuple(segments), has_rope=has_rope),
        grid=(m // tm,),
        in_specs=in_specs,
        out_specs=out_specs,
        out_shape=out_shape,
        compiler_params=_params("parallel"),
    )(*args)


def _ffn_kernel(*refs, final_norm, n_side):
    x_ref, g_ref, wg_ref, wu_ref, wd_ref = refs[:5]
    refs = refs[5:]
    if final_norm:
        gf_ref, refs = refs[0], refs[1:]
    side_in, refs = refs[:n_side], refs[n_side:]
    o_ref, refs = refs[0], refs[1:]
    side_out, refs = refs[:n_side], refs[n_side:]
    h_ref, = refs
    j = pl.program_id(1)

    @pl.when(j == 0)
    def _():
        x = x_ref[...]
        h_ref[...] = _rms(x, g_ref[...]).astype(BF16)
        o_ref[...] = x

    for src, dst in zip(side_in, side_out):
        dst[...] = src[...].astype(dst.dtype)

    h = h_ref[...]
    gate = jnp.dot(h, wg_ref[...], preferred_element_type=F32)
    up = jnp.dot(h, wu_ref[...], preferred_element_type=F32)
    act = (gate * jax.nn.sigmoid(gate) * up * 0.5).astype(BF16)
    o_ref[...] += jnp.dot(act, wd_ref[...], preferred_element_type=F32)

    if final_norm:
        @pl.when(j == pl.num_programs(1) - 1)
        def _():
            o_ref[...] = _rms(o_ref[...], gf_ref[...])


def _side_blocks(n_rows, n_steps):
    for nb in range(n_steps, 0, -1):
        if n_rows % nb == 0 and (n_rows // nb) % BF16_SUBLANES == 0:
            return nb
    raise ValueError(f"no aligned row split of {n_rows} rows over {n_steps} steps")


def _ffn(x, g, w_up, w_down, *, tm, tf, final_gain=None, side=()):
    m, d = x.shape
    f = w_down.shape[0]
    nf = f // tf
    assert m % tm == 0 and f % tf == 0 and w_up.shape[1] == 2 * f
    n_steps = (m // tm) * nf
    in_specs = [
        pl.BlockSpec((tm, d), lambda i, j: (i, 0)),
        pl.BlockSpec((1, d), lambda i, j: (0, 0)),
        pl.BlockSpec((d, tf), lambda i, j: (0, j)),
        pl.BlockSpec((d, tf), lambda i, j: (0, nf + j)),
        pl.BlockSpec((tf, d), lambda i, j: (j, 0)),
    ]
    args = [x, g.reshape(1, d), w_up, w_up, w_down]
    if final_gain is not None:
        in_specs.append(pl.BlockSpec((1, d), lambda i, j: (0, 0)))
        args.append(final_gain.reshape(1, d))
    out_specs = [pl.BlockSpec((tm, d), lambda i, j: (i, 0))]
    out_shape = [jax.ShapeDtypeStruct((m, d), F32)]
    for w3, layer in side:
        _, rows, cols = w3.shape
        nb = _side_blocks(rows, n_steps)
        rb = rows // nb
        in_specs.append(pl.BlockSpec((None, rb, cols),
                                     lambda i, j, nb=nb, layer=layer: (layer, jnp.minimum(i * nf + j, nb - 1), 0)))
        out_specs.append(pl.BlockSpec((rb, cols), lambda i, j, nb=nb: (jnp.minimum(i * nf + j, nb - 1), 0)))
        out_shape.append(jax.ShapeDtypeStruct((rows, cols), BF16))
        args.append(w3)
    outs = pl.pallas_call(
        functools.partial(_ffn_kernel, final_norm=final_gain is not None, n_side=len(side)),
        grid=(m // tm, nf),
        in_specs=in_specs,
        out_specs=out_specs,
        out_shape=out_shape,
        scratch_shapes=[pltpu.VMEM((tm, d), BF16)],
        compiler_params=_params("arbitrary", "arbitrary"),
    )(*args)
    return outs[0], list(outs[1:])


def _conv_mixer_kernel(x_ref, g_ref, w_ref, cw_ref, mix_ref, qmem_ref, st_ref, carry_ref, ubuf,
                       *, tiles_per_seq):
    tm = x_ref.shape[0]
    cdim = mix_ref.shape[1]
    chunk = ubuf.shape[1]
    i = pl.program_id(0)

    @pl.when(i % tiles_per_seq == 0)
    def _():
        carry_ref[...] = jnp.zeros(carry_ref.shape, F32)

    h = _rms(x_ref[...], g_ref[...]).astype(BF16)
    cw = cw_ref[...]
    for c0 in range(0, cdim, chunk):
        cs = slice(c0, c0 + chunk)
        b_gate = jnp.dot(h, w_ref[:, c0:c0 + chunk], preferred_element_type=F32)
        c_gate = jnp.dot(h, w_ref[:, cdim + c0:cdim + c0 + chunk], preferred_element_type=F32)
        x_in = jnp.dot(h, w_ref[:, 2 * cdim + c0:2 * cdim + c0 + chunk], preferred_element_type=F32)
        u = c_gate * x_in
        ubuf[0:8, :] = carry_ref[:, cs]
        ubuf[8:tm + 8, :] = u
        conv = cw[0:1, cs] * ubuf[6:tm + 6, :] + cw[1:2, cs] * ubuf[7:tm + 7, :] + cw[2:3, cs] * u
        mix_ref[:, cs] = (b_gate * conv).astype(mix_ref.dtype)
        st_ref[:, cs] = u[tm - (CONV_WIDTH - 1):tm]
        carry_ref[:, cs] = u[tm - 8:tm]
    qmem_ref[...] = jnp.dot(h, w_ref[:, 3 * cdim:], preferred_element_type=F32).astype(qmem_ref.dtype)


def _conv_mixer_prompt(x, g, w, conv_w, *, bsz, tm, cdim):
    m, d = x.shape
    tiles_per_seq = m // bsz // tm
    n_mem_q = w.shape[1] - 3 * cdim
    return pl.pallas_call(
        functools.partial(_conv_mixer_kernel, tiles_per_seq=tiles_per_seq),
        grid=(m // tm,),
        in_specs=[pl.BlockSpec((tm, d), lambda i: (i, 0)),
                  pl.BlockSpec((1, d), lambda i: (0, 0)),
                  pl.BlockSpec(w.shape, lambda i: (0, 0), pipeline_mode=pl.Buffered(1)),
                  pl.BlockSpec((CONV_WIDTH, cdim), lambda i: (0, 0))],
        out_specs=[pl.BlockSpec((tm, cdim), lambda i: (i, 0)),
                   pl.BlockSpec((tm, n_mem_q), lambda i: (i, 0)),
                   pl.BlockSpec((None, CONV_WIDTH - 1, cdim), lambda i: (i // tiles_per_seq, 0, 0))],
        out_shape=[jax.ShapeDtypeStruct((m, cdim), BF16),
                   jax.ShapeDtypeStruct((m, n_mem_q), BF16),
                   jax.ShapeDtypeStruct((bsz, CONV_WIDTH - 1, cdim), F32)],
        scratch_shapes=[pltpu.VMEM((8, cdim), F32), pltpu.VMEM((tm + 8, PROJ_CHUNK), F32)],
        compiler_params=_params("arbitrary"),
    )(x, g.reshape(1, d), w, conv_w)


def _conv_gate_step_kernel(b_ref, c_ref, xin_ref, s0_ref, s1_ref, w_ref, mix_ref, u_ref):
    u = c_ref[...].astype(F32) * xin_ref[...].astype(F32)
    w = w_ref[...]
    conv = w[0:1] * s0_ref[...] + w[1:2] * s1_ref[...] + w[2:3] * u
    mix_ref[...] = (b_ref[...].astype(F32) * conv).astype(mix_ref.dtype)
    u_ref[...] = u


def _conv_gate_step(proj, s0, s1, conv_w, *, cdim):
    m = proj.shape[0]
    col = lambda cb: pl.BlockSpec((m, cdim), lambda i: (0, cb))
    full = lambda r: pl.BlockSpec((r, cdim), lambda i: (0, 0))
    return pl.pallas_call(
        _conv_gate_step_kernel,
        grid=(1,),
        in_specs=[col(0), col(1), col(2), full(m), full(m), full(CONV_WIDTH)],
        out_specs=[full(m), full(m)],
        out_shape=[jax.ShapeDtypeStruct((m, cdim), BF16), jax.ShapeDtypeStruct((m, cdim), F32)],
        compiler_params=_params("arbitrary"),
    )(proj, proj, proj, s0, s1, conv_w)


def _mem_attn_kernel(q_ref, k_ref, v_ref, o_ref):
    rows = q_ref.shape[0]
    q = q_ref[...].astype(BF16)
    if rows < 8:
        q = jnp.broadcast_to(q[0:1], (8, q.shape[1]))
    k = k_ref[...].astype(BF16)
    v = v_ref[...].astype(BF16)
    scale = MEM_HEAD_DIM ** -0.5
    for h in range(MEM_HEADS):
        hs = slice(h * MEM_HEAD_DIM, (h + 1) * MEM_HEAD_DIM)
        s = lax.dot_general(q[:, hs], k[:, hs], (((1,), (1,)), ((), ())),
                            preferred_element_type=F32) * scale
        p = jnp.exp(s - jnp.max(s, axis=-1, keepdims=True))
        l = jnp.sum(p, axis=-1, keepdims=True)
        o = jnp.dot(p.astype(BF16), v[:, hs], preferred_element_type=F32) / l
        o_ref[:, hs] = o[:rows].astype(o_ref.dtype)


def _mem_attn(q3, k3, v3, *, tq):
    nb, rows, width = q3.shape
    n_mem = k3.shape[1]
    return pl.pallas_call(
        _mem_attn_kernel,
        grid=(nb, rows // tq),
        in_specs=[pl.BlockSpec((None, tq, width), lambda b, i: (b, i, 0)),
                  pl.BlockSpec((None, n_mem, width), lambda b, i: (b, 0, 0)),
                  pl.BlockSpec((None, n_mem, width), lambda b, i: (b, 0, 0))],
        out_specs=pl.BlockSpec((None, tq, width), lambda b, i: (b, i, 0)),
        out_shape=jax.ShapeDtypeStruct((nb, rows, width), BF16),
        compiler_params=_params("parallel", "parallel"),
    )(q3, k3, v3)


def _out_proj_kernel(a1_ref, a2_ref, w_ref, x_ref, o_ref):
    n1 = a1_ref.shape[1]
    acc = jnp.dot(a1_ref[...], w_ref[0:n1, :], preferred_element_type=F32)
    acc = acc + jnp.dot(a2_ref[...], w_ref[n1:, :], preferred_element_type=F32)
    o_ref[...] = x_ref[...] + acc


def _out_proj(a1, a2, w, x, *, tm):
    m, d = x.shape
    n1, n2 = a1.shape[1], a2.shape[1]
    return pl.pallas_call(
        _out_proj_kernel,
        grid=(m // tm,),
        in_specs=[pl.BlockSpec((tm, n1), lambda i: (i, 0)),
                  pl.BlockSpec((tm, n2), lambda i: (i, 0)),
                  pl.BlockSpec((n1 + n2, d), lambda i: (0, 0), pipeline_mode=pl.Buffered(1)),
                  pl.BlockSpec((tm, d), lambda i: (i, 0))],
        out_specs=pl.BlockSpec((tm, d), lambda i: (i, 0)),
        out_shape=jax.ShapeDtypeStruct((m, d), F32),
        compiler_params=_params("parallel"),
    )(a1, a2, w, x)


def _lambda_value(lq1_ref, lk1_ref, lq2_ref, lk2_ref, lam_init):
    a = jnp.exp(jnp.sum(lq1_ref[...] * lk1_ref[...], axis=-1, keepdims=True))
    b = jnp.exp(jnp.sum(lq2_ref[...] * lk2_ref[...], axis=-1, keepdims=True))
    return a - b + lam_init


def _subln(d, gain, lam_init):
    return _rms(d, gain) * (1.0 - lam_init)


def _diff_attn_kernel(q_ref, k_ref, v_ref, lq1_ref, lk1_ref, lq2_ref, lk2_ref, g_ref, o_ref,
                      q2_ref, vt_ref, m_ref, l_ref, acc_ref, *, lam_init):
    tq = q_ref.shape[0]
    tk = vt_ref.shape[2]
    qi = pl.program_id(2)

    @pl.when(qi == 0)
    def _():
        for c in range(vt_ref.shape[0]):
            vt_ref[c] = v_ref[c * tk:(c + 1) * tk, :].T

    q = q_ref[...]
    lane = lax.broadcasted_iota(jnp.int32, q.shape, 1)
    zero = jnp.zeros_like(q)
    q2_ref[0:tq, :] = jnp.where(lane < DIFF_HEAD_DIM, q, zero)
    q2_ref[tq:2 * tq, :] = jnp.where(lane < DIFF_HEAD_DIM, zero, q)
    m_ref[...] = jnp.full(m_ref.shape, -jnp.inf, F32)
    l_ref[...] = jnp.zeros(l_ref.shape, F32)
    acc_ref[...] = jnp.zeros(acc_ref.shape, F32)

    chains = range(0, 2 * tq, ATTN_CHAIN_LANES)

    def n_keys(c, masked):
        return (c % tq) + ATTN_CHAIN_LANES if masked else tk

    def scores(kj, masked):
        start = pl.multiple_of(kj * tk, tk)
        k = k_ref[pl.ds(start, tk), :]
        return [lax.dot_general(k[0:n_keys(c, masked)], q2_ref[c:c + ATTN_CHAIN_LANES, :],
                                (((1,), (1,)), ((), ())), preferred_element_type=F32) for c in chains]

    def absorb(kj, sts, masked):
        vt = vt_ref[kj]
        for c, st in zip(chains, sts):
            cs = slice(c, c + ATTN_CHAIN_LANES)
            if masked:
                key = lax.broadcasted_iota(jnp.int32, st.shape, 0)
                qry = lax.broadcasted_iota(jnp.int32, st.shape, 1) + (c % tq)
                st = jnp.where(key <= qry, st, -jnp.inf)
            m_prev = m_ref[:, cs]
            m_next = jnp.maximum(m_prev, jnp.max(st, axis=0, keepdims=True))
            alpha = jnp.exp2(m_prev - m_next)
            p = jnp.exp2(st - m_next)
            l_ref[:, cs] = alpha * l_ref[:, cs] + jnp.sum(p, axis=0, keepdims=True)
            acc_ref[:, cs] = alpha * acc_ref[:, cs] + jnp.dot(vt[:, 0:n_keys(c, masked)], p.astype(BF16),
                                                              preferred_element_type=F32)
            m_ref[:, cs] = m_next

    def pair(t, carry):
        sa, sb = scores(2 * t, False), scores(2 * t + 1, False)
        absorb(2 * t, sa, False)
        absorb(2 * t + 1, sb, False)
        return carry

    lax.fori_loop(0, qi // 2, pair, 0)

    @pl.when(qi % 2 == 0)
    def _():
        absorb(qi, scores(qi, True), True)

    @pl.when(qi % 2 == 1)
    def _():
        sa, sd = scores(qi - 1, False), scores(qi, True)
        absorb(qi - 1, sa, False)
        absorb(qi, sd, True)

    ot = acc_ref[...] / l_ref[...]
    lam = _lambda_value(lq1_ref, lk1_ref, lq2_ref, lk2_ref, lam_init)
    dt = ot[:, 0:tq] - lam * ot[:, tq:2 * tq]
    yt = dt * lax.rsqrt(jnp.mean(dt * dt, axis=0, keepdims=True) + NORM_EPS)
    o_ref[...] = (yt.T * g_ref[...] * (1.0 - lam_init)).astype(o_ref.dtype)


def _diff_attn_prompt(q, k, v4, lam_params, gain, *, bsz, seq, tq, lam_init):
    m = bsz * seq
    width = k.shape[1]
    heads = width // LANES
    nq = seq // tq
    small = lambda a: pl.BlockSpec(a.shape, lambda b, h, i: (0, 0))
    lam_args = [a.reshape(1, -1) for a in lam_params]
    gain = gain.reshape(1, -1)
    return pl.pallas_call(
        functools.partial(_diff_attn_kernel, lam_init=lam_init),
        grid=(bsz, heads, nq),
        in_specs=[pl.BlockSpec((tq, LANES), lambda b, h, i: (b * nq + i, h)),
                  pl.BlockSpec((seq, LANES), lambda b, h, i: (b, h)),
                  pl.BlockSpec((None, None, seq, LANES), lambda b, h, i: (b, h, 0, 0))]
                 + [small(a) for a in lam_args] + [small(gain)],
        out_specs=pl.BlockSpec((tq, LANES), lambda b, h, i: (b * nq + i, h)),
        out_shape=jax.ShapeDtypeStruct((m, width), BF16),
        scratch_shapes=[pltpu.VMEM((2 * tq, LANES), BF16),
                        pltpu.VMEM((seq // tq, LANES, tq), BF16),
                        pltpu.VMEM((1, 2 * tq), F32),
                        pltpu.VMEM((1, 2 * tq), F32),
                        pltpu.VMEM((LANES, 2 * tq), F32)],
        compiler_params=_params("parallel", "parallel", "arbitrary"),
    )(q, k, v4, *lam_args, gain)


def _decode_attn_kernel(pt_ref, q_ref, kn_ref, vn_ref, *refs, n_pages, lam_init):
    del pt_ref
    k_refs, v_refs = refs[:n_pages], refs[n_pages:2 * n_pages]
    lq1_ref, lk1_ref, lq2_ref, lk2_ref, g_ref, o_ref, qbd_ref, m_ref, l_ref, acc_ref = refs[2 * n_pages:]
    g = pl.program_id(1)
    width = q_ref.shape[1]

    @pl.when(g == 0)
    def _():
        row = lax.broadcasted_iota(jnp.int32, (DECODE_ROWS, width), 0)
        lane = lax.broadcasted_iota(jnp.int32, (DECODE_ROWS, width), 1)
        qb = jnp.broadcast_to(q_ref[...], (DECODE_ROWS, width))
        keep = (lane >= row * DIFF_HEAD_DIM) & (lane < (row + 1) * DIFF_HEAD_DIM)
        qbd_ref[...] = jnp.where(keep, qb, 0.0).astype(BF16)
        m_ref[...] = jnp.full(m_ref.shape, -jnp.inf, F32)
        l_ref[...] = jnp.zeros(l_ref.shape, F32)
        acc_ref[...] = jnp.zeros(acc_ref.shape, F32)

    qbd = qbd_ref[...]
    s = jnp.concatenate(
        [jnp.dot(qbd, kr[...].astype(BF16), preferred_element_type=F32) for kr in k_refs], axis=1)
    m_prev = m_ref[:, 0:1]
    m_next = jnp.maximum(m_prev, jnp.max(s, axis=-1, keepdims=True))
    alpha = jnp.exp(m_prev - m_next)
    p = jnp.exp(s - m_next)
    l_next = alpha * l_ref[:, 0:1] + jnp.sum(p, axis=-1, keepdims=True)
    pb = p.astype(BF16)
    page = k_refs[0].shape[1]
    for h in range(width // LANES):
        hs = slice(h * LANES, (h + 1) * LANES)
        pv = jnp.dot(pb[:, 0:page], v_refs[0][h].astype(BF16), preferred_element_type=F32)
        for r in range(1, n_pages):
            pv = pv + jnp.dot(pb[:, r * page:(r + 1) * page], v_refs[r][h].astype(BF16),
                              preferred_element_type=F32)
        acc_ref[:, hs] = alpha * acc_ref[:, hs] + pv
    m_ref[...] = jnp.broadcast_to(m_next, m_ref.shape)
    l_ref[...] = jnp.broadcast_to(l_next, l_ref.shape)

    @pl.when(g == pl.num_programs(1) - 1)
    def _():
        kn = kn_ref[...].astype(BF16).astype(F32)
        s_new = jnp.sum(qbd_ref[...].astype(F32) * kn, axis=-1, keepdims=True)
        m_old = m_ref[:, 0:1]
        m_fin = jnp.maximum(m_old, s_new)
        a = jnp.exp(m_old - m_fin)
        p_new = jnp.exp(s_new - m_fin)
        l_fin = a * l_ref[:, 0:1] + p_new
        o = (a * acc_ref[...] + p_new * vn_ref[...]) / l_fin
        lam = _lambda_value(lq1_ref, lk1_ref, lq2_ref, lk2_ref, lam_init)
        gain = g_ref[...]
        for h in range(width // LANES):
            hs = slice(h * LANES, (h + 1) * LANES)
            d = o[2 * h:2 * h + 1, hs] - lam * o[2 * h + 1:2 * h + 2, hs]
            o_ref[:, hs] = _subln(d, gain, lam_init).astype(o_ref.dtype)


def _decode_attn(q, k_new, v_new, cache_k, cache_v, page_table, lam_params, gain, *, lam_init):
    nb, _, width = q.shape
    page = cache_k.shape[2]
    n_used = page_table.shape[1]
    npg = DECODE_PAGES_PER_STEP
    assert n_used % npg == 0
    row = pl.BlockSpec((None, 1, width), lambda b, g, pt: (b, 0, 0))
    small = lambda a: pl.BlockSpec(a.shape, lambda b, g, pt: (0, 0))
    k_spec = lambda r: pl.BlockSpec((None, width, page),
                                    lambda b, g, pt: (pt[b, g * npg + r], 0, 0))
    v_spec = lambda r: pl.BlockSpec((None, width // LANES, page, LANES),
                                    lambda b, g, pt: (pt[b, g * npg + r], 0, 0, 0))
    lam_args = [a.reshape(1, -1) for a in lam_params]
    gain = gain.reshape(1, -1)
    grid_spec = pltpu.PrefetchScalarGridSpec(
        num_scalar_prefetch=1,
        grid=(nb, n_used // npg),
        in_specs=[row, row, row] + [k_spec(r) for r in range(npg)] + [v_spec(r) for r in range(npg)]
                 + [small(a) for a in lam_args] + [small(gain)],
        out_specs=pl.BlockSpec((None, 1, width), lambda b, g, pt: (b, 0, 0)),
        scratch_shapes=[pltpu.VMEM((DECODE_ROWS, width), BF16),
                        pltpu.VMEM((DECODE_ROWS, LANES), F32),
                        pltpu.VMEM((DECODE_ROWS, LANES), F32),
                        pltpu.VMEM((DECODE_ROWS, width), F32)],
    )
    return pl.pallas_call(
        functools.partial(_decode_attn_kernel, n_pages=npg, lam_init=lam_init),
        grid_spec=grid_spec,
        out_shape=jax.ShapeDtypeStruct((nb, 1, width), BF16),
        compiler_params=_params("parallel", "arbitrary"),
    )(page_table, q, k_new, v_new, *([cache_k] * npg), *([cache_v] * npg), *lam_args, gain)


def _rope_tables(pos):
    half = DIFF_HEAD_DIM // 2
    inv_freq = ROPE_THETA ** (-jnp.arange(half, dtype=F32) / half)
    ang = pos.astype(F32)[:, None] * inv_freq[None, :]
    cos, sin, zero = jnp.cos(ang), jnp.sin(ang), jnp.zeros_like(ang)
    reps = LANES // DIFF_HEAD_DIM
    cos_t = jnp.tile(jnp.concatenate([cos, cos], axis=1), (1, reps))
    sin_lo = jnp.tile(jnp.concatenate([-sin, zero], axis=1), (1, reps))
    sin_hi = jnp.tile(jnp.concatenate([zero, sin], axis=1), (1, reps))
    return cos_t, sin_lo, sin_hi


def _trunk(x, p, wb, side_plan, *, bsz, seq, tiles, rope, mem_k, mem_v, conv_state, cache, depth):
    m, d = x.shape
    n_a = depth // 2
    prompt = cache is None
    cdim = p['conv_w'].shape[-1]
    qk_width = p['qk_width']
    mem_width = MEM_HEADS * MEM_HEAD_DIM
    tm = tiles.proj
    act = BF16 if prompt else F32
    new_conv = []
    k_out = v_out = k_b = v_b = None

    def ffn(name, l, x, final_gain=None):
        jobs = side_plan.get((name, l), [])
        y, copies = _ffn(x, p['norm_' + name][l], wb[('w_' + name + '_up', l)], wb[('w_' + name + '_down', l)],
                         tm=tiles.ffn, tf=tiles.ffn_cols, final_gain=final_gain,
                         side=[(w3, layer) for _, w3, layer in jobs])
        for (key, _, _), copy in zip(jobs, copies):
            wb[key] = copy
        return y

    for l in range(depth):
        if l == n_a:
            if prompt:
                k_out, k_b, v_out, v_b = _norm_proj(
                    x, p['norm_kv'], wb[('w_kv', 0)],
                    [_Segment(qk_width, (_Out(F32, 'cols'), _Out(BF16)), rope=True),
                     _Segment(qk_width, (_Out(F32, 'heads'), _Out(BF16, 'heads')))],
                    tm=tm, seq=seq, rope=rope)
            else:
                k_out, v_out = _norm_proj(
                    x, p['norm_kv'], wb[('w_kv', 0)],
                    [_Segment(qk_width, (_Out(F32),), rope=True), _Segment(qk_width, (_Out(F32),))],
                    tm=tm, seq=seq, rope=rope)
        x = ffn('ffn1', l, x)
        if l < n_a:
            if prompt:
                mix, q_mem, st = _conv_mixer_prompt(x, p['norm_mix'][l], wb[('w_in_a', l)], p['conv_w'][l],
                                                    bsz=bsz, tm=tm, cdim=cdim)
            else:
                proj, q_mem = _norm_proj(x, p['norm_mix'][l], wb[('w_in_a', l)],
                                         [_Segment(3 * cdim, (_Out(act),)), _Segment(mem_width, (_Out(act),))],
                                         tm=tm, seq=seq)
                s0, s1 = conv_state[l]
                mix, u = _conv_gate_step(proj, s0, s1, p['conv_w'][l], cdim=cdim)
                st = jnp.stack([s1, u], axis=1)
            new_conv.append(st)
        else:
            j = l - n_a
            lam_init = 0.8 - 0.6 * math.exp(-0.3 * l)
            lam_params = (p['lambda_q1'][j], p['lambda_k1'][j], p['lambda_q2'][j], p['lambda_k2'][j])
            q_scale = DIFF_HEAD_DIM ** -0.5 * (LOG2E if prompt else 1.0)
            q, q_mem = _norm_proj(
                x, p['norm_mix'][l], wb[('w_in_b', j)],
                [_Segment(qk_width, (_Out(act),), rope=True, scale=q_scale),
                 _Segment(mem_width, (_Out(act),))],
                tm=tm, seq=seq, rope=rope)
            if prompt:
                mix = _diff_attn_prompt(q, k_b, v_b, lam_params, p['subln_gain'][j],
                                        bsz=bsz, seq=seq, tq=tiles.attn, lam_init=lam_init)
            else:
                cache_k, cache_v, page_table = cache
                mix = _decode_attn(q.reshape(bsz, 1, qk_width),
                                   k_out.reshape(bsz, 1, qk_width), v_out.reshape(bsz, 1, qk_width),
                                   cache_k, cache_v, page_table, lam_params, p['subln_gain'][j],
                                   lam_init=lam_init).reshape(m, qk_width)
        mem_o = _mem_attn(q_mem.reshape(bsz, seq, mem_width), mem_k[l], mem_v[l], tq=min(tm, seq))
        x = _out_proj(mix, mem_o.reshape(m, mem_width), wb[('w_out', l)], x, tm=tm)
        x = ffn('ffn2', l, x, final_gain=p['norm_final'] if l == depth - 1 else None)
    return x, new_conv, k_out, v_out


def kernel(x_prompt, x_sample, state_conv, cache_k, cache_v, cache_mem_k, cache_mem_v, page_table, mem_prompt, norm_ffn1, w_ffn1_up, w_ffn1_down, norm_mix, w_in_a, conv_w, w_in_b, lambda_q1, lambda_k1, lambda_q2, lambda_k2, subln_gain, norm_mem, w_mem_kv, w_out, norm_ffn2, w_ffn2_up, w_ffn2_down, norm_kv, w_kv, norm_final):
    bsz_p, s_p, d = x_prompt.shape
    bsz_s, s_s, _ = x_sample.shape
    depth = norm_ffn1.shape[0]
    n_mem = mem_prompt.shape[1]
    mem_width = MEM_HEADS * MEM_HEAD_DIM
    n_pool, page = cache_k.shape[0], cache_k.shape[1]
    k_heads, v_heads = cache_k.shape[2], cache_v.shape[2]
    qk_width = k_heads * cache_k.shape[3]
    past_len = page_table.shape[1] * page
    assert s_s == 1

    n_a = depth // 2
    p = {
        'norm_ffn1': norm_ffn1, 'norm_ffn2': norm_ffn2, 'norm_mix': norm_mix, 'conv_w': conv_w,
        'lambda_q1': lambda_q1, 'lambda_k1': lambda_k1, 'lambda_q2': lambda_q2, 'lambda_k2': lambda_k2,
        'subln_gain': subln_gain, 'norm_kv': norm_kv, 'norm_final': norm_final, 'qk_width': qk_width,
    }
    stacked = {'w_ffn1_up': w_ffn1_up, 'w_ffn1_down': w_ffn1_down, 'w_ffn2_up': w_ffn2_up,
               'w_ffn2_down': w_ffn2_down, 'w_in_a': w_in_a, 'w_in_b': w_in_b, 'w_out': w_out,
               'w_kv': w_kv[None]}

    def job(name, layer):
        return ((name, layer), stacked[name], layer)

    first = [job('w_ffn1_up', 0), job('w_ffn1_down', 0)] + ([job('w_kv', 0)] if n_a == 0 else [])
    wb = {key: _cast_layer(w3, layer) for key, w3, layer in first}
    side_plan = {}
    for l in range(depth):
        mixer = job('w_in_a', l) if l < n_a else job('w_in_b', l - n_a)
        side_plan[('ffn1', l)] = [mixer, job('w_out', l), job('w_ffn2_up', l), job('w_ffn2_down', l)]
        if l + 1 < depth:
            shared_kv = [job('w_kv', 0)] if l + 1 == n_a else []
            side_plan[('ffn2', l)] = shared_kv + [job('w_ffn1_up', l + 1), job('w_ffn1_down', l + 1)]

    mem_rows = mem_prompt.reshape(bsz_p * n_mem, d)
    mem_kv = [_norm_proj(mem_rows, norm_mem[l], _cast_layer(w_mem_kv, l),
                         [_Segment(mem_width, (_Out(F32),)), _Segment(mem_width, (_Out(F32),))],
                         tm=n_mem, seq=n_mem)
              for l in range(depth)]
    mem_k_p = [kv[0].reshape(bsz_p, n_mem, mem_width) for kv in mem_kv]
    mem_v_p = [kv[1].reshape(bsz_p, n_mem, mem_width) for kv in mem_kv]
    y_p, conv_p, k_t, v_h = _trunk(
        x_prompt.reshape(bsz_p * s_p, d), p, wb, side_plan, bsz=bsz_p, seq=s_p, tiles=PROMPT_TILES,
        rope=_rope_tables(jnp.arange(s_p)), mem_k=mem_k_p, mem_v=mem_v_p,
        conv_state=None, cache=None, depth=depth)

    rows_s = bsz_s * s_s
    pos_s = jnp.full((rows_s,), past_len, jnp.int32)
    cmk = cache_mem_k.reshape(depth, bsz_s, n_mem, mem_width)
    cmv = cache_mem_v.reshape(depth, bsz_s, n_mem, mem_width)
    y_s, conv_s, k_s, v_s = _trunk(
        x_sample.reshape(rows_s, d), p, wb, {}, bsz=bsz_s, seq=s_s,
        tiles=_Tiles(proj=rows_s, ffn=rows_s, ffn_cols=PROMPT_TILES.ffn_cols, attn=0),
        rope=_rope_tables(pos_s),
        mem_k=[cmk[l] for l in range(depth)], mem_v=[cmv[l] for l in range(depth)],
        conv_state=[(state_conv[l, :, 0], state_conv[l, :, 1]) for l in range(n_a)],
        cache=(jnp.transpose(cache_k, (0, 2, 3, 1)).reshape(n_pool, qk_width, page),
               jnp.transpose(cache_v, (0, 2, 1, 3)), page_table),
        depth=depth)

    mem_shape = (depth, bsz_p, n_mem, MEM_HEADS, MEM_HEAD_DIM)
    return (y_p.reshape(bsz_p, s_p, d),
            y_s.reshape(bsz_s, s_s, d),
            jnp.stack(conv_p, axis=0),
            jnp.stack(conv_s, axis=0),
            jnp.transpose(k_t.reshape(bsz_p, k_heads, -1, s_p), (0, 3, 1, 2)),
            jnp.transpose(v_h, (0, 2, 1, 3)),
            k_s.reshape(bsz_s, s_s, k_heads, -1),
            v_s.reshape(bsz_s, s_s, v_heads, -1),
            jnp.stack(mem_k_p, axis=0).reshape(mem_shape),
            jnp.stack(mem_v_p, axis=0).reshape(mem_shape))
```

```python
import functools
import math
from typing import NamedTuple

import jax
import jax.numpy as jnp
from jax import lax
from jax.experimental import pallas as pl
from jax.experimental.pallas import tpu as pltpu

F32 = jnp.float32
BF16 = jnp.bfloat16

NORM_EPS = 1e-6
ROPE_THETA = 10000.0
LOG2E = math.log2(math.e)
LANES = 128
BF16_SUBLANES = 16
VMEM_LIMIT = 56 * 1024 * 1024
PROJ_CHUNK = 512
CAST_BLOCK_BYTES = 8 * 1024 * 1024

MEM_HEADS = 4
MEM_HEAD_DIM = 128
DIFF_HEAD_DIM = 64
CONV_WIDTH = 3
ATTN_CHAIN_LANES = 256
DECODE_PAGES_PER_STEP = 8
DECODE_ROWS = 32


class _Tiles(NamedTuple):
    proj: int
    ffn: int
    ffn_cols: int
    attn: int


PROMPT_TILES = _Tiles(proj=512, ffn=512, ffn_cols=512, attn=512)


def _params(*sem):
    return pltpu.CompilerParams(dimension_semantics=sem, vmem_limit_bytes=VMEM_LIMIT)


def _rms(x, g):
    ms = jnp.mean(x * x, axis=-1, keepdims=True)
    return x * lax.rsqrt(ms + NORM_EPS) * g


def _cast_kernel(w_ref, o_ref):
    o_ref[...] = w_ref[...].astype(o_ref.dtype)


def _cast_layer(w3, layer):
    _, k, n = w3.shape
    rows = k
    while rows * n * 4 > CAST_BLOCK_BYTES and rows % 16 == 0:
        rows //= 2
    return pl.pallas_call(
        _cast_kernel,
        grid=(k // rows,),
        in_specs=[pl.BlockSpec((None, rows, n), lambda i: (layer, i, 0))],
        out_specs=pl.BlockSpec((rows, n), lambda i: (i, 0)),
        out_shape=jax.ShapeDtypeStruct((k, n), BF16),
        compiler_params=_params("parallel"),
    )(w3)


def _rope_tile(y, cos, sin_lo, sin_hi):
    outs = []
    for c in range(y.shape[1] // LANES):
        yc = y[:, c * LANES:(c + 1) * LANES]
        up = pltpu.roll(yc, LANES - 32, axis=1)
        dn = pltpu.roll(yc, 32, axis=1)
        outs.append(yc * cos + up * sin_lo + dn * sin_hi)
    return jnp.concatenate(outs, axis=1) if len(outs) > 1 else outs[0]


class _Out(NamedTuple):
    dtype: object
    layout: str = 'rows'


class _Segment(NamedTuple):
    n_cols: int
    outs: tuple
    rope: bool = False
    scale: float = 1.0


def _norm_proj_kernel(*refs, segments, has_rope):
    if has_rope:
        x_ref, g_ref, w_ref, cos_ref, slo_ref, shi_ref = refs[:6]
        out_refs = refs[6:]
    else:
        x_ref, g_ref, w_ref = refs[:3]
        out_refs = refs[3:]
    h = _rms(x_ref[...], g_ref[...]).astype(BF16)
    col, k = 0, 0
    for seg in segments:
        outs = out_refs[k:k + len(seg.outs)]
        for c0 in range(0, seg.n_cols, PROJ_CHUNK):
            c1 = min(c0 + PROJ_CHUNK, seg.n_cols)
            y = jnp.dot(h, w_ref[:, col + c0:col + c1].astype(BF16), preferred_element_type=F32)
            if seg.rope:
                y = _rope_tile(y, cos_ref[...], slo_ref[...], shi_ref[...]) * seg.scale
            for o, spec in zip(outs, seg.outs):
                if spec.layout == 'rows':
                    o[:, c0:c1] = y.astype(o.dtype)
                elif spec.layout == 'cols':
                    o[c0:c1, :] = y.T.astype(o.dtype)
                else:
                    for hh in range(c0 // LANES, c1 // LANES):
                        o[hh] = y[:, hh * LANES - c0:(hh + 1) * LANES - c0].astype(o.dtype)
        col += seg.n_cols
        k += len(seg.outs)


def _norm_proj(x, g, w, segments, *, tm, seq, rope=None, layer=None):
    m, d = x.shape
    assert m % tm == 0 and sum(s.n_cols for s in segments) == w.shape[-1]
    has_rope = any(s.rope for s in segments)
    row_major = all(o.layout == 'rows' for s in segments for o in s.outs)
    assert row_major or seq % tm == 0
    per_seq = max(seq // tm, 1)
    if layer is None:
        w_spec = pl.BlockSpec(w.shape, lambda i: (0, 0), pipeline_mode=pl.Buffered(1))
    else:
        w_spec = pl.BlockSpec((None,) + w.shape[1:], lambda i: (layer, 0, 0), pipeline_mode=pl.Buffered(1))
    in_specs = [
        pl.BlockSpec((tm, d), lambda i: (i, 0)),
        pl.BlockSpec((1, d), lambda i: (0, 0)),
        w_spec,
    ]
    args = [x, g.reshape(1, d), w]
    if has_rope:
        n_tab = rope[0].shape[0] // tm
        for t in rope:
            in_specs.append(pl.BlockSpec((tm, LANES), lambda i: (i % n_tab, 0)))
            args.append(t)
    out_specs, out_shape = [], []
    for seg in segments:
        n = seg.n_cols
        for o in seg.outs:
            if o.layout == 'rows':
                out_specs.append(pl.BlockSpec((tm, n), lambda i: (i, 0)))
                out_shape.append(jax.ShapeDtypeStruct((m, n), o.dtype))
            elif o.layout == 'cols':
                out_specs.append(pl.BlockSpec((None, n, tm), lambda i: (i // per_seq, 0, i % per_seq)))
                out_shape.append(jax.ShapeDtypeStruct((m // seq, n, seq), o.dtype))
            else:
                out_specs.append(pl.BlockSpec((None, n // LANES, tm, LANES),
                                              lambda i: (i // per_seq, 0, i % per_seq, 0)))
                out_shape.append(jax.ShapeDtypeStruct((m // seq, n // LANES, seq, LANES), o.dtype))
    return pl.pallas_call(
        functools.partial(_norm_proj_kernel, segments=tuple(segments), has_rope=has_rope),
        grid=(m // tm,),
        in_specs=in_specs,
        out_specs=out_specs,
        out_shape=out_shape,
        compiler_params=_params("parallel"),
    )(*args)


def _ffn_kernel(*refs, final_norm, n_side):
    x_ref, g_ref, wg_ref, wu_ref, wd_ref = refs[:5]
    refs = refs[5:]
    if final_norm:
        gf_ref, refs = refs[0], refs[1:]
    side_in, refs = refs[:n_side], refs[n_side:]
    o_ref, refs = refs[0], refs[1:]
    side_out, refs = refs[:n_side], refs[n_side:]
    h_ref, = refs
    j = pl.program_id(1)

    @pl.when(j == 0)
    def _():
        x = x_ref[...]
        h_ref[...] = _rms(x, g_ref[...]).astype(BF16)
        o_ref[...] = x

    for src, dst in zip(side_in, side_out):
        dst[...] = src[...].astype(dst.dtype)

    h = h_ref[...]
    gate = jnp.dot(h, wg_ref[...], preferred_element_type=F32)
    up = jnp.dot(h, wu_ref[...], preferred_element_type=F32)
    act = (gate * jax.nn.sigmoid(gate) * up * 0.5).astype(BF16)
    o_ref[...] += jnp.dot(act, wd_ref[...], preferred_element_type=F32)

    if final_norm:
        @pl.when(j == pl.num_programs(1) - 1)
        def _():
            o_ref[...] = _rms(o_ref[...], gf_ref[...])


def _side_blocks(n_rows, n_steps):
    for nb in range(n_steps, 0, -1):
        if n_rows % nb == 0 and (n_rows // nb) % BF16_SUBLANES == 0:
            return nb
    raise ValueError(f"no aligned row split of {n_rows} rows over {n_steps} steps")


def _ffn(x, g, w_up, w_down, *, tm, tf, final_gain=None, side=()):
    m, d = x.shape
    f = w_down.shape[0]
    nf = f // tf
    assert m % tm == 0 and f % tf == 0 and w_up.shape[1] == 2 * f
    n_steps = (m // tm) * nf
    in_specs = [
        pl.BlockSpec((tm, d), lambda i, j: (i, 0)),
        pl.BlockSpec((1, d), lambda i, j: (0, 0)),
        pl.BlockSpec((d, tf), lambda i, j: (0, j)),
        pl.BlockSpec((d, tf), lambda i, j: (0, nf + j)),
        pl.BlockSpec((tf, d), lambda i, j: (j, 0)),
    ]
    args = [x, g.reshape(1, d), w_up, w_up, w_down]
    if final_gain is not None:
        in_specs.append(pl.BlockSpec((1, d), lambda i, j: (0, 0)))
        args.append(final_gain.reshape(1, d))
    out_specs = [pl.BlockSpec((tm, d), lambda i, j: (i, 0))]
    out_shape = [jax.ShapeDtypeStruct((m, d), F32)]
    for w3, layer in side:
        _, rows, cols = w3.shape
        nb = _side_blocks(rows, n_steps)
        rb = rows // nb
        in_specs.append(pl.BlockSpec((None, rb, cols),
                                     lambda i, j, nb=nb, layer=layer: (layer, jnp.minimum(i * nf + j, nb - 1), 0)))
        out_specs.append(pl.BlockSpec((rb, cols), lambda i, j, nb=nb: (jnp.minimum(i * nf + j, nb - 1), 0)))
        out_shape.append(jax.ShapeDtypeStruct((rows, cols), BF16))
        args.append(w3)
    outs = pl.pallas_call(
        functools.partial(_ffn_kernel, final_norm=final_gain is not None, n_side=len(side)),
        grid=(m // tm, nf),
        in_specs=in_specs,
        out_specs=out_specs,
        out_shape=out_shape,
        scratch_shapes=[pltpu.VMEM((tm, d), BF16)],
        compiler_params=_params("arbitrary", "arbitrary"),
    )(*args)
    return outs[0], list(outs[1:])


def _conv_mixer_kernel(x_ref, g_ref, w_ref, cw_ref, mix_ref, qmem_ref, st_ref, carry_ref, ubuf,
                       *, tiles_per_seq):
    tm = x_ref.shape[0]
    cdim = mix_ref.shape[1]
    chunk = ubuf.shape[1]
    i = pl.program_id(0)

    @pl.when(i % tiles_per_seq == 0)
    def _():
        carry_ref[...] = jnp.zeros(carry_ref.shape, F32)

    h = _rms(x_ref[...], g_ref[...]).astype(BF16)
    cw = cw_ref[...]
    for c0 in range(0, cdim, chunk):
        cs = slice(c0, c0 + chunk)
        b_gate = jnp.dot(h, w_ref[:, c0:c0 + chunk], preferred_element_type=F32)
        c_gate = jnp.dot(h, w_ref[:, cdim + c0:cdim + c0 + chunk], preferred_element_type=F32)
        x_in = jnp.dot(h, w_ref[:, 2 * cdim + c0:2 * cdim + c0 + chunk], preferred_element_type=F32)
        u = c_gate * x_in
        ubuf[0:8, :] = carry_ref[:, cs]
        ubuf[8:tm + 8, :] = u
        conv = cw[0:1, cs] * ubuf[6:tm + 6, :] + cw[1:2, cs] * ubuf[7:tm + 7, :] + cw[2:3, cs] * u
        mix_ref[:, cs] = (b_gate * conv).astype(mix_ref.dtype)
        st_ref[:, cs] = u[tm - (CONV_WIDTH - 1):tm]
        carry_ref[:, cs] = u[tm - 8:tm]
    qmem_ref[...] = jnp.dot(h, w_ref[:, 3 * cdim:], preferred_element_type=F32).astype(qmem_ref.dtype)


def _conv_mixer_prompt(x, g, w, conv_w, *, bsz, tm, cdim):
    m, d = x.shape
    tiles_per_seq = m // bsz // tm
    n_mem_q = w.shape[1] - 3 * cdim
    return pl.pallas_call(
        functools.partial(_conv_mixer_kernel, tiles_per_seq=tiles_per_seq),
        grid=(m // tm,),
        in_specs=[pl.BlockSpec((tm, d), lambda i: (i, 0)),
                  pl.BlockSpec((1, d), lambda i: (0, 0)),
                  pl.BlockSpec(w.shape, lambda i: (0, 0), pipeline_mode=pl.Buffered(1)),
                  pl.BlockSpec((CONV_WIDTH, cdim), lambda i: (0, 0))],
        out_specs=[pl.BlockSpec((tm, cdim), lambda i: (i, 0)),
                   pl.BlockSpec((tm, n_mem_q), lambda i: (i, 0)),
                   pl.BlockSpec((None, CONV_WIDTH - 1, cdim), lambda i: (i // tiles_per_seq, 0, 0))],
        out_shape=[jax.ShapeDtypeStruct((m, cdim), BF16),
                   jax.ShapeDtypeStruct((m, n_mem_q), BF16),
                   jax.ShapeDtypeStruct((bsz, CONV_WIDTH - 1, cdim), F32)],
        scratch_shapes=[pltpu.VMEM((8, cdim), F32), pltpu.VMEM((tm + 8, PROJ_CHUNK), F32)],
        compiler_params=_params("arbitrary"),
    )(x, g.reshape(1, d), w, conv_w)


def _conv_gate_step_kernel(b_ref, c_ref, xin_ref, s0_ref, s1_ref, w_ref, mix_ref, u_ref):
    u = c_ref[...].astype(F32) * xin_ref[...].astype(F32)
    w = w_ref[...]
    conv = w[0:1] * s0_ref[...] + w[1:2] * s1_ref[...] + w[2:3] * u
    mix_ref[...] = (b_ref[...].astype(F32) * conv).astype(mix_ref.dtype)
    u_ref[...] = u


def _conv_gate_step(proj, s0, s1, conv_w, *, cdim):
    m = proj.shape[0]
    col = lambda cb: pl.BlockSpec((m, cdim), lambda i: (0, cb))
    full = lambda r: pl.BlockSpec((r, cdim), lambda i: (0, 0))
    return pl.pallas_call(
        _conv_gate_step_kernel,
        grid=(1,),
        in_specs=[col(0), col(1), col(2), full(m), full(m), full(CONV_WIDTH)],
        out_specs=[full(m), full(m)],
        out_shape=[jax.ShapeDtypeStruct((m, cdim), BF16), jax.ShapeDtypeStruct((m, cdim), F32)],
        compiler_params=_params("arbitrary"),
    )(proj, proj, proj, s0, s1, conv_w)


def _mem_attention(q_ref, k_ref, v_ref):
    rows = q_ref.shape[0]
    q = q_ref[...].astype(BF16)
    if rows < 8:
        q = jnp.broadcast_to(q[0:1], (8, q.shape[1]))
    k = k_ref[...].astype(BF16)
    v = v_ref[...].astype(BF16)
    scale = MEM_HEAD_DIM ** -0.5
    outs = []
    for h in range(MEM_HEADS):
        hs = slice(h * MEM_HEAD_DIM, (h + 1) * MEM_HEAD_DIM)
        s = lax.dot_general(q[:, hs], k[:, hs], (((1,), (1,)), ((), ())),
                            preferred_element_type=F32) * scale
        p = jnp.exp(s - jnp.max(s, axis=-1, keepdims=True))
        l = jnp.sum(p, axis=-1, keepdims=True)
        o = jnp.dot(p.astype(BF16), v[:, hs], preferred_element_type=F32) / l
        outs.append(o[:rows])
    return outs


def _mem_attn_kernel(q_ref, k_ref, v_ref, o_ref):
    for h, o in enumerate(_mem_attention(q_ref, k_ref, v_ref)):
        o_ref[:, h * MEM_HEAD_DIM:(h + 1) * MEM_HEAD_DIM] = o.astype(o_ref.dtype)


def _mem_attn(q3, k3, v3, *, tq):
    nb, rows, width = q3.shape
    n_mem = k3.shape[1]
    return pl.pallas_call(
        _mem_attn_kernel,
        grid=(nb, rows // tq),
        in_specs=[pl.BlockSpec((None, tq, width), lambda b, i: (b, i, 0)),
                  pl.BlockSpec((None, n_mem, width), lambda b, i: (b, 0, 0)),
                  pl.BlockSpec((None, n_mem, width), lambda b, i: (b, 0, 0))],
        out_specs=pl.BlockSpec((None, tq, width), lambda b, i: (b, i, 0)),
        out_shape=jax.ShapeDtypeStruct((nb, rows, width), BF16),
        compiler_params=_params("parallel", "parallel"),
    )(q3, k3, v3)


def _out_proj_kernel(*refs, fused_mem):
    if fused_mem:
        a1_ref, q_ref, k_ref, v_ref, w_ref, x_ref, o_ref = refs
        a2 = jnp.concatenate([o.astype(BF16) for o in _mem_attention(q_ref, k_ref, v_ref)], axis=1)
    else:
        a1_ref, a2_ref, w_ref, x_ref, o_ref = refs
        a2 = a2_ref[...]
    n1 = a1_ref.shape[1]
    acc = jnp.dot(a1_ref[...], w_ref[0:n1, :], preferred_element_type=F32)
    acc = acc + jnp.dot(a2, w_ref[n1:, :], preferred_element_type=F32)
    o_ref[...] = x_ref[...] + acc


def _out_proj(a1, a2, w, x, *, tm, mem=None):
    m, d = x.shape
    n1, n2 = a1.shape[1], a2.shape[1]
    row = lambda n: pl.BlockSpec((tm, n), lambda i: (i, 0))
    in_specs, args = [row(n1), row(n2)], [a1, a2]
    if mem is not None:
        k3, v3, seq = mem
        per_seq = seq // tm
        for kv in (k3, v3):
            in_specs.append(pl.BlockSpec((None,) + kv.shape[1:], lambda i: (i // per_seq, 0, 0)))
            args.append(kv)
    in_specs += [pl.BlockSpec((n1 + n2, d), lambda i: (0, 0), pipeline_mode=pl.Buffered(1)), row(d)]
    args += [w, x]
    return pl.pallas_call(
        functools.partial(_out_proj_kernel, fused_mem=mem is not None),
        grid=(m // tm,),
        in_specs=in_specs,
        out_specs=row(d),
        out_shape=jax.ShapeDtypeStruct((m, d), F32),
        compiler_params=_params("parallel"),
    )(*args)


def _lambda_value(lq1_ref, lk1_ref, lq2_ref, lk2_ref, lam_init):
    a = jnp.exp(jnp.sum(lq1_ref[...] * lk1_ref[...], axis=-1, keepdims=True))
    b = jnp.exp(jnp.sum(lq2_ref[...] * lk2_ref[...], axis=-1, keepdims=True))
    return a - b + lam_init


def _subln(d, gain, lam_init):
    return _rms(d, gain) * (1.0 - lam_init)


def _diff_attn_kernel(q_ref, k_ref, v_ref, lq1_ref, lk1_ref, lq2_ref, lk2_ref, g_ref, o_ref,
                      q2_ref, vt_ref, m_ref, l_ref, acc_ref, *, lam_init):
    tq = q_ref.shape[0]
    tk = vt_ref.shape[2]
    qi = pl.program_id(2)

    @pl.when(qi == 0)
    def _():
        for c in range(vt_ref.shape[0]):
            vt_ref[c] = v_ref[c * tk:(c + 1) * tk, :].T

    q = q_ref[...]
    lane = lax.broadcasted_iota(jnp.int32, q.shape, 1)
    zero = jnp.zeros_like(q)
    q2_ref[0:tq, :] = jnp.where(lane < DIFF_HEAD_DIM, q, zero)
    q2_ref[tq:2 * tq, :] = jnp.where(lane < DIFF_HEAD_DIM, zero, q)
    m_ref[...] = jnp.full(m_ref.shape, -jnp.inf, F32)
    l_ref[...] = jnp.zeros(l_ref.shape, F32)
    acc_ref[...] = jnp.zeros(acc_ref.shape, F32)

    chains = range(0, 2 * tq, ATTN_CHAIN_LANES)

    def n_keys(c, masked):
        return (c % tq) + ATTN_CHAIN_LANES if masked else tk

    def scores(kj, masked):
        start = pl.multiple_of(kj * tk, tk)
        k = k_ref[pl.ds(start, tk), :]
        return [lax.dot_general(k[0:n_keys(c, masked)], q2_ref[c:c + ATTN_CHAIN_LANES, :],
                                (((1,), (1,)), ((), ())), preferred_element_type=F32) for c in chains]

    def absorb(kj, sts, masked):
        vt = vt_ref[kj]
        for c, st in zip(chains, sts):
            cs = slice(c, c + ATTN_CHAIN_LANES)
            if masked:
                key = lax.broadcasted_iota(jnp.int32, st.shape, 0)
                qry = lax.broadcasted_iota(jnp.int32, st.shape, 1) + (c % tq)
                st = jnp.where(key <= qry, st, -jnp.inf)
            m_prev = m_ref[:, cs]
            m_next = jnp.maximum(m_prev, jnp.max(st, axis=0, keepdims=True))
            alpha = jnp.exp2(m_prev - m_next)
            p = jnp.exp2(st - m_next)
            l_ref[:, cs] = alpha * l_ref[:, cs] + jnp.sum(p, axis=0, keepdims=True)
            acc_ref[:, cs] = alpha * acc_ref[:, cs] + jnp.dot(vt[:, 0:n_keys(c, masked)], p.astype(BF16),
                                                              preferred_element_type=F32)
            m_ref[:, cs] = m_next

    def full_blocks(first, count):
        sts = [scores(first + u, False) for u in range(count)]
        for u, st in enumerate(sts):
            absorb(first + u, st, False)

    def quad(t, carry):
        full_blocks(4 * t, 4)
        return carry

    lax.fori_loop(0, qi // 4, quad, 0)

    @pl.when((qi & 2) != 0)
    def _():
        full_blocks((qi // 4) * 4, 2)

    @pl.when((qi & 1) == 0)
    def _():
        absorb(qi, scores(qi, True), True)

    @pl.when((qi & 1) == 1)
    def _():
        sa, sd = scores(qi - 1, False), scores(qi, True)
        absorb(qi - 1, sa, False)
        absorb(qi, sd, True)

    ot = acc_ref[...] / l_ref[...]
    lam = _lambda_value(lq1_ref, lk1_ref, lq2_ref, lk2_ref, lam_init)
    dt = ot[:, 0:tq] - lam * ot[:, tq:2 * tq]
    yt = dt * lax.rsqrt(jnp.mean(dt * dt, axis=0, keepdims=True) + NORM_EPS)
    o_ref[...] = (yt.T * g_ref[...] * (1.0 - lam_init)).astype(o_ref.dtype)


def _diff_attn_prompt(q, k, v4, lam_params, gain, *, bsz, seq, tq, lam_init):
    m = bsz * seq
    width = k.shape[1]
    heads = width // LANES
    nq = seq // tq
    small = lambda a: pl.BlockSpec(a.shape, lambda b, h, i: (0, 0))
    lam_args = [a.reshape(1, -1) for a in lam_params]
    gain = gain.reshape(1, -1)
    return pl.pallas_call(
        functools.partial(_diff_attn_kernel, lam_init=lam_init),
        grid=(bsz, heads, nq),
        in_specs=[pl.BlockSpec((tq, LANES), lambda b, h, i: (b * nq + i, h)),
                  pl.BlockSpec((seq, LANES), lambda b, h, i: (b, h)),
                  pl.BlockSpec((None, None, seq, LANES), lambda b, h, i: (b, h, 0, 0))]
                 + [small(a) for a in lam_args] + [small(gain)],
        out_specs=pl.BlockSpec((tq, LANES), lambda b, h, i: (b * nq + i, h)),
        out_shape=jax.ShapeDtypeStruct((m, width), BF16),
        scratch_shapes=[pltpu.VMEM((2 * tq, LANES), BF16),
                        pltpu.VMEM((seq // tq, LANES, tq), BF16),
                        pltpu.VMEM((1, 2 * tq), F32),
                        pltpu.VMEM((1, 2 * tq), F32),
                        pltpu.VMEM((LANES, 2 * tq), F32)],
        compiler_params=_params("parallel", "parallel", "arbitrary"),
    )(q, k, v4, *lam_args, gain)


def _decode_attn_kernel(pt_ref, q_ref, kn_ref, vn_ref, *refs, n_pages, lam_init):
    del pt_ref
    k_refs, v_refs = refs[:n_pages], refs[n_pages:2 * n_pages]
    lq1_ref, lk1_ref, lq2_ref, lk2_ref, g_ref, o_ref, qbd_ref, m_ref, l_ref, acc_ref = refs[2 * n_pages:]
    g = pl.program_id(1)
    width = q_ref.shape[1]

    @pl.when(g == 0)
    def _():
        row = lax.broadcasted_iota(jnp.int32, (DECODE_ROWS, width), 0)
        lane = lax.broadcasted_iota(jnp.int32, (DECODE_ROWS, width), 1)
        qb = jnp.broadcast_to(q_ref[...], (DECODE_ROWS, width))
        keep = (lane >= row * DIFF_HEAD_DIM) & (lane < (row + 1) * DIFF_HEAD_DIM)
        qbd_ref[...] = jnp.where(keep, qb, 0.0).astype(BF16)
        m_ref[...] = jnp.full(m_ref.shape, -jnp.inf, F32)
        l_ref[...] = jnp.zeros(l_ref.shape, F32)
        acc_ref[...] = jnp.zeros(acc_ref.shape, F32)

    qbd = qbd_ref[...]
    s = jnp.concatenate(
        [jnp.dot(qbd, kr[...].astype(BF16), preferred_element_type=F32) for kr in k_refs], axis=1)
    m_prev = m_ref[:, 0:1]
    m_next = jnp.maximum(m_prev, jnp.max(s, axis=-1, keepdims=True))
    alpha = jnp.exp(m_prev - m_next)
    p = jnp.exp(s - m_next)
    l_next = alpha * l_ref[:, 0:1] + jnp.sum(p, axis=-1, keepdims=True)
    pb = p.astype(BF16)
    page = k_refs[0].shape[1]
    for h in range(width // LANES):
        hs = slice(h * LANES, (h + 1) * LANES)
        pv = jnp.dot(pb[:, 0:page], v_refs[0][h].astype(BF16), preferred_element_type=F32)
        for r in range(1, n_pages):
            pv = pv + jnp.dot(pb[:, r * page:(r + 1) * page], v_refs[r][h].astype(BF16),
                              preferred_element_type=F32)
        acc_ref[:, hs] = alpha * acc_ref[:, hs] + pv
    m_ref[...] = jnp.broadcast_to(m_next, m_ref.shape)
    l_ref[...] = jnp.broadcast_to(l_next, l_ref.shape)

    @pl.when(g == pl.num_programs(1) - 1)
    def _():
        kn = kn_ref[...].astype(BF16).astype(F32)
        s_new = jnp.sum(qbd_ref[...].astype(F32) * kn, axis=-1, keepdims=True)
        m_old = m_ref[:, 0:1]
        m_fin = jnp.maximum(m_old, s_new)
        a = jnp.exp(m_old - m_fin)
        p_new = jnp.exp(s_new - m_fin)
        l_fin = a * l_ref[:, 0:1] + p_new
        o = (a * acc_ref[...] + p_new * vn_ref[...]) / l_fin
        lam = _lambda_value(lq1_ref, lk1_ref, lq2_ref, lk2_ref, lam_init)
        gain = g_ref[...]
        for h in range(width // LANES):
            hs = slice(h * LANES, (h + 1) * LANES)
            d = o[2 * h:2 * h + 1, hs] - lam * o[2 * h + 1:2 * h + 2, hs]
            o_ref[:, hs] = _subln(d, gain, lam_init).astype(o_ref.dtype)


def _decode_attn(q, k_new, v_new, cache_k, cache_v, page_table, lam_params, gain, *, lam_init):
    nb, _, width = q.shape
    page = cache_k.shape[2]
    n_used = page_table.shape[1]
    npg = DECODE_PAGES_PER_STEP
    assert n_used % npg == 0
    row = pl.BlockSpec((None, 1, width), lambda b, g, pt: (b, 0, 0))
    small = lambda a: pl.BlockSpec(a.shape, lambda b, g, pt: (0, 0))
    k_spec = lambda r: pl.BlockSpec((None, width, page),
                                    lambda b, g, pt: (pt[b, g * npg + r], 0, 0))
    v_spec = lambda r: pl.BlockSpec((None, width // LANES, page, LANES),
                                    lambda b, g, pt: (pt[b, g * npg + r], 0, 0, 0))
    lam_args = [a.reshape(1, -1) for a in lam_params]
    gain = gain.reshape(1, -1)
    grid_spec = pltpu.PrefetchScalarGridSpec(
        num_scalar_prefetch=1,
        grid=(nb, n_used // npg),
        in_specs=[row, row, row] + [k_spec(r) for r in range(npg)] + [v_spec(r) for r in range(npg)]
                 + [small(a) for a in lam_args] + [small(gain)],
        out_specs=pl.BlockSpec((None, 1, width), lambda b, g, pt: (b, 0, 0)),
        scratch_shapes=[pltpu.VMEM((DECODE_ROWS, width), BF16),
                        pltpu.VMEM((DECODE_ROWS, LANES), F32),
                        pltpu.VMEM((DECODE_ROWS, LANES), F32),
                        pltpu.VMEM((DECODE_ROWS, width), F32)],
    )
    return pl.pallas_call(
        functools.partial(_decode_attn_kernel, n_pages=npg, lam_init=lam_init),
        grid_spec=grid_spec,
        out_shape=jax.ShapeDtypeStruct((nb, 1, width), BF16),
        compiler_params=_params("parallel", "arbitrary"),
    )(page_table, q, k_new, v_new, *([cache_k] * npg), *([cache_v] * npg), *lam_args, gain)


def _rope_tables(pos):
    half = DIFF_HEAD_DIM // 2
    inv_freq = ROPE_THETA ** (-jnp.arange(half, dtype=F32) / half)
    ang = pos.astype(F32)[:, None] * inv_freq[None, :]
    cos, sin, zero = jnp.cos(ang), jnp.sin(ang), jnp.zeros_like(ang)
    reps = LANES // DIFF_HEAD_DIM
    cos_t = jnp.tile(jnp.concatenate([cos, cos], axis=1), (1, reps))
    sin_lo = jnp.tile(jnp.concatenate([-sin, zero], axis=1), (1, reps))
    sin_hi = jnp.tile(jnp.concatenate([zero, sin], axis=1), (1, reps))
    return cos_t, sin_lo, sin_hi


def _trunk(x, p, wb, side_plan, *, bsz, seq, tiles, rope, mem_k, mem_v, conv_state, cache, depth):
    m, d = x.shape
    n_a = depth // 2
    prompt = cache is None
    cdim = p['conv_w'].shape[-1]
    qk_width = p['qk_width']
    mem_width = MEM_HEADS * MEM_HEAD_DIM
    tm = tiles.proj
    act = BF16 if prompt else F32
    new_conv = []
    k_out = v_out = k_b = v_b = None

    def ffn(name, l, x, final_gain=None):
        jobs = side_plan.get((name, l), [])
        y, copies = _ffn(x, p['norm_' + name][l], wb[('w_' + name + '_up', l)], wb[('w_' + name + '_down', l)],
                         tm=tiles.ffn, tf=tiles.ffn_cols, final_gain=final_gain,
                         side=[(w3, layer) for _, w3, layer in jobs])
        for (key, _, _), copy in zip(jobs, copies):
            wb[key] = copy
        return y

    for l in range(depth):
        if l == n_a:
            if prompt:
                k_out, k_b, v_out, v_b = _norm_proj(
                    x, p['norm_kv'], wb[('w_kv', 0)],
                    [_Segment(qk_width, (_Out(F32, 'cols'), _Out(BF16)), rope=True),
                     _Segment(qk_width, (_Out(F32, 'heads'), _Out(BF16, 'heads')))],
                    tm=tm, seq=seq, rope=rope)
            else:
                k_out, v_out = _norm_proj(
                    x, p['norm_kv'], wb[('w_kv', 0)],
                    [_Segment(qk_width, (_Out(F32),), rope=True), _Segment(qk_width, (_Out(F32),))],
                    tm=tm, seq=seq, rope=rope)
        x = ffn('ffn1', l, x)
        if l < n_a:
            if prompt:
                mix, q_mem, st = _conv_mixer_prompt(x, p['norm_mix'][l], wb[('w_in_a', l)], p['conv_w'][l],
                                                    bsz=bsz, tm=tm, cdim=cdim)
            else:
                proj, q_mem = _norm_proj(x, p['norm_mix'][l], wb[('w_in_a', l)],
                                         [_Segment(3 * cdim, (_Out(act),)), _Segment(mem_width, (_Out(act),))],
                                         tm=tm, seq=seq)
                s0, s1 = conv_state[l]
                mix, u = _conv_gate_step(proj, s0, s1, p['conv_w'][l], cdim=cdim)
                st = jnp.stack([s1, u], axis=1)
            new_conv.append(st)
        else:
            j = l - n_a
            lam_init = 0.8 - 0.6 * math.exp(-0.3 * l)
            lam_params = (p['lambda_q1'][j], p['lambda_k1'][j], p['lambda_q2'][j], p['lambda_k2'][j])
            q_scale = DIFF_HEAD_DIM ** -0.5 * (LOG2E if prompt else 1.0)
            q, q_mem = _norm_proj(
                x, p['norm_mix'][l], wb[('w_in_b', j)],
                [_Segment(qk_width, (_Out(act),), rope=True, scale=q_scale),
                 _Segment(mem_width, (_Out(act),))],
                tm=tm, seq=seq, rope=rope)
            if prompt:
                mix = _diff_attn_prompt(q, k_b, v_b, lam_params, p['subln_gain'][j],
                                        bsz=bsz, seq=seq, tq=tiles.attn, lam_init=lam_init)
            else:
                cache_k, cache_v, page_table = cache
                mix = _decode_attn(q.reshape(bsz, 1, qk_width),
                                   k_out.reshape(bsz, 1, qk_width), v_out.reshape(bsz, 1, qk_width),
                                   cache_k, cache_v, page_table, lam_params, p['subln_gain'][j],
                                   lam_init=lam_init).reshape(m, qk_width)
        if prompt:
            x = _out_proj(mix, q_mem, wb[('w_out', l)], x, tm=tm, mem=(mem_k[l], mem_v[l], seq))
        else:
            mem_o = _mem_attn(q_mem.reshape(bsz, seq, mem_width), mem_k[l], mem_v[l], tq=seq)
            x = _out_proj(mix, mem_o.reshape(m, mem_width), wb[('w_out', l)], x, tm=tm)
        x = ffn('ffn2', l, x, final_gain=p['norm_final'] if l == depth - 1 else None)
    return x, new_conv, k_out, v_out


def kernel(x_prompt, x_sample, state_conv, cache_k, cache_v, cache_mem_k, cache_mem_v, page_table, mem_prompt, norm_ffn1, w_ffn1_up, w_ffn1_down, norm_mix, w_in_a, conv_w, w_in_b, lambda_q1, lambda_k1, lambda_q2, lambda_k2, subln_gain, norm_mem, w_mem_kv, w_out, norm_ffn2, w_ffn2_up, w_ffn2_down, norm_kv, w_kv, norm_final):
    bsz_p, s_p, d = x_prompt.shape
    bsz_s, s_s, _ = x_sample.shape
    depth = norm_ffn1.shape[0]
    n_mem = mem_prompt.shape[1]
    mem_width = MEM_HEADS * MEM_HEAD_DIM
    n_pool, page = cache_k.shape[0], cache_k.shape[1]
    k_heads, v_heads = cache_k.shape[2], cache_v.shape[2]
    qk_width = k_heads * cache_k.shape[3]
    past_len = page_table.shape[1] * page
    assert s_s == 1

    n_a = depth // 2
    p = {
        'norm_ffn1': norm_ffn1, 'norm_ffn2': norm_ffn2, 'norm_mix': norm_mix, 'conv_w': conv_w,
        'lambda_q1': lambda_q1, 'lambda_k1': lambda_k1, 'lambda_q2': lambda_q2, 'lambda_k2': lambda_k2,
        'subln_gain': subln_gain, 'norm_kv': norm_kv, 'norm_final': norm_final, 'qk_width': qk_width,
    }
    stacked = {'w_ffn1_up': w_ffn1_up, 'w_ffn1_down': w_ffn1_down, 'w_ffn2_up': w_ffn2_up,
               'w_ffn2_down': w_ffn2_down, 'w_in_a': w_in_a, 'w_in_b': w_in_b, 'w_out': w_out,
               'w_kv': w_kv[None]}

    def job(name, layer):
        return ((name, layer), stacked[name], layer)

    first = [job('w_ffn1_up', 0), job('w_ffn1_down', 0)] + ([job('w_kv', 0)] if n_a == 0 else [])
    wb = {key: _cast_layer(w3, layer) for key, w3, layer in first}
    side_plan = {}
    for l in range(depth):
        mixer = job('w_in_a', l) if l < n_a else job('w_in_b', l - n_a)
        side_plan[('ffn1', l)] = [mixer, job('w_out', l), job('w_ffn2_up', l), job('w_ffn2_down', l)]
        if l + 1 < depth:
            shared_kv = [job('w_kv', 0)] if l + 1 == n_a else []
            side_plan[('ffn2', l)] = shared_kv + [job('w_ffn1_up', l + 1), job('w_ffn1_down', l + 1)]

    mem_rows = mem_prompt.reshape(bsz_p * n_mem, d)
    mem_kv = [_norm_proj(mem_rows, norm_mem[l], w_mem_kv,
                         [_Segment(mem_width, (_Out(F32),)), _Segment(mem_width, (_Out(F32),))],
                         tm=bsz_p * n_mem, seq=n_mem, layer=l)
              for l in range(depth)]
    mem_k_p = [kv[0].reshape(bsz_p, n_mem, mem_width) for kv in mem_kv]
    mem_v_p = [kv[1].reshape(bsz_p, n_mem, mem_width) for kv in mem_kv]
    y_p, conv_p, k_t, v_h = _trunk(
        x_prompt.reshape(bsz_p * s_p, d), p, wb, side_plan, bsz=bsz_p, seq=s_p, tiles=PROMPT_TILES,
        rope=_rope_tables(jnp.arange(s_p)), mem_k=mem_k_p, mem_v=mem_v_p,
        conv_state=None, cache=None, depth=depth)

    rows_s = bsz_s * s_s
    pos_s = jnp.full((rows_s,), past_len, jnp.int32)
    cmk = cache_mem_k.reshape(depth, bsz_s, n_mem, mem_width)
    cmv = cache_mem_v.reshape(depth, bsz_s, n_mem, mem_width)
    y_s, conv_s, k_s, v_s = _trunk(
        x_sample.reshape(rows_s, d), p, wb, {}, bsz=bsz_s, seq=s_s,
        tiles=_Tiles(proj=rows_s, ffn=rows_s, ffn_cols=PROMPT_TILES.ffn_cols, attn=0),
        rope=_rope_tables(pos_s),
        mem_k=[cmk[l] for l in range(depth)], mem_v=[cmv[l] for l in range(depth)],
        conv_state=[(state_conv[l, :, 0], state_conv[l, :, 1]) for l in range(n_a)],
        cache=(jnp.transpose(cache_k, (0, 2, 3, 1)).reshape(n_pool, qk_width, page),
               jnp.transpose(cache_v, (0, 2, 1, 3)), page_table),
        depth=depth)

    mem_shape = (depth, bsz_p, n_mem, MEM_HEADS, MEM_HEAD_DIM)
    return (y_p.reshape(bsz_p, s_p, d),
            y_s.reshape(bsz_s, s_s, d),
            jnp.stack(conv_p, axis=0),
            jnp.stack(conv_s, axis=0),
            jnp.transpose(k_t.reshape(bsz_p, k_heads, -1, s_p), (0, 3, 1, 2)),
            jnp.transpose(v_h, (0, 2, 1, 3)),
            k_s.reshape(bsz_s, s_s, k_heads, -1),
            v_s.reshape(bsz_s, s_s, v_heads, -1),
            jnp.stack(mem_k_p, axis=0).reshape(mem_shape),
            jnp.stack(mem_v_p, axis=0).reshape(mem_shape))
```

```python
import functools
import math
from typing import NamedTuple

import jax
import jax.numpy as jnp
from jax import lax
from jax.experimental import pallas as pl
from jax.experimental.pallas import tpu as pltpu

F32 = jnp.float32
BF16 = jnp.bfloat16

NORM_EPS = 1e-6
ROPE_THETA = 10000.0
LOG2E = math.log2(math.e)
LANES = 128
BF16_SUBLANES = 16
VMEM_LIMIT = 56 * 1024 * 1024
PROJ_CHUNK = 512
CAST_BLOCK_BYTES = 8 * 1024 * 1024

MEM_HEADS = 4
MEM_HEAD_DIM = 128
DIFF_HEAD_DIM = 64
ROPE_HALF = DIFF_HEAD_DIM // 2
CONV_WIDTH = 3
ATTN_CHAIN_LANES = 256
DECODE_PAGES_PER_STEP = 8
DECODE_ROWS = 32


class _Tiles(NamedTuple):
    proj: int
    ffn: int
    ffn_cols: int
    attn: int


PROMPT_TILES = _Tiles(proj=512, ffn=512, ffn_cols=512, attn=512)
SAMPLE_FFN_COLS = 1408


def _params(*sem):
    return pltpu.CompilerParams(dimension_semantics=sem, vmem_limit_bytes=VMEM_LIMIT)


def _rms(x, g):
    ms = jnp.mean(x * x, axis=-1, keepdims=True)
    return x * lax.rsqrt(ms + NORM_EPS) * g


def _cast_kernel(w_ref, o_ref):
    o_ref[...] = w_ref[...].astype(o_ref.dtype)


def _cast_layer(w3, layer):
    _, k, n = w3.shape
    rows = k
    while rows * n * 4 > CAST_BLOCK_BYTES and rows % 16 == 0:
        rows //= 2
    return pl.pallas_call(
        _cast_kernel,
        grid=(k // rows,),
        in_specs=[pl.BlockSpec((None, rows, n), lambda i: (layer, i, 0))],
        out_specs=pl.BlockSpec((rows, n), lambda i: (i, 0)),
        out_shape=jax.ShapeDtypeStruct((k, n), BF16),
        compiler_params=_params("parallel"),
    )(w3)


def _rope_tile(y, cos, sin_lo, sin_hi):
    outs = []
    for c in range(y.shape[1] // LANES):
        yc = y[:, c * LANES:(c + 1) * LANES]
        up = pltpu.roll(yc, LANES - ROPE_HALF, axis=1)
        dn = pltpu.roll(yc, ROPE_HALF, axis=1)
        outs.append(yc * cos + up * sin_lo + dn * sin_hi)
    return jnp.concatenate(outs, axis=1) if len(outs) > 1 else outs[0]


class _Out(NamedTuple):
    dtype: object
    layout: str = 'rows'


class _Segment(NamedTuple):
    n_cols: int
    outs: tuple
    rope: bool = False
    scale: float = 1.0


def _norm_proj_kernel(*refs, segments, has_rope):
    if has_rope:
        x_ref, g_ref, w_ref, cos_ref, slo_ref, shi_ref = refs[:6]
        out_refs = refs[6:]
    else:
        x_ref, g_ref, w_ref = refs[:3]
        out_refs = refs[3:]
    h = _rms(x_ref[...], g_ref[...]).astype(BF16)
    col, k = 0, 0
    for seg in segments:
        outs = out_refs[k:k + len(seg.outs)]
        for c0 in range(0, seg.n_cols, PROJ_CHUNK):
            c1 = min(c0 + PROJ_CHUNK, seg.n_cols)
            y = jnp.dot(h, w_ref[:, col + c0:col + c1].astype(BF16), preferred_element_type=F32)
            if seg.rope:
                y = _rope_tile(y, cos_ref[...], slo_ref[...], shi_ref[...]) * seg.scale
            for o, spec in zip(outs, seg.outs):
                if spec.layout == 'rows':
                    o[:, c0:c1] = y.astype(o.dtype)
                elif spec.layout == 'cols':
                    o[c0:c1, :] = y.T.astype(o.dtype)
                else:
                    for hh in range(c0 // LANES, c1 // LANES):
                        o[hh] = y[:, hh * LANES - c0:(hh + 1) * LANES - c0].astype(o.dtype)
        col += seg.n_cols
        k += len(seg.outs)


def _norm_proj(x, g, w, segments, *, tm, seq, rope=None, layer=None):
    m, d = x.shape
    assert m % tm == 0 and sum(s.n_cols for s in segments) == w.shape[-1]
    has_rope = any(s.rope for s in segments)
    row_major = all(o.layout == 'rows' for s in segments for o in s.outs)
    assert row_major or seq % tm == 0
    per_seq = max(seq // tm, 1)
    if layer is None:
        w_spec = pl.BlockSpec(w.shape, lambda i: (0, 0), pipeline_mode=pl.Buffered(1))
    else:
        w_spec = pl.BlockSpec((None,) + w.shape[1:], lambda i: (layer, 0, 0), pipeline_mode=pl.Buffered(1))
    in_specs = [
        pl.BlockSpec((tm, d), lambda i: (i, 0)),
        pl.BlockSpec((1, d), lambda i: (0, 0)),
        w_spec,
    ]
    args = [x, g.reshape(1, d), w]
    if has_rope:
        n_tab = rope[0].shape[0] // tm
        for t in rope:
            in_specs.append(pl.BlockSpec((tm, LANES), lambda i: (i % n_tab, 0)))
            args.append(t)
    out_specs, out_shape = [], []
    for seg in segments:
        n = seg.n_cols
        for o in seg.outs:
            if o.layout == 'rows':
                out_specs.append(pl.BlockSpec((tm, n), lambda i: (i, 0)))
                out_shape.append(jax.ShapeDtypeStruct((m, n), o.dtype))
            elif o.layout == 'cols':
                out_specs.append(pl.BlockSpec((None, n, tm), lambda i: (i // per_seq, 0, i % per_seq)))
                out_shape.append(jax.ShapeDtypeStruct((m // seq, n, seq), o.dtype))
            else:
                out_specs.append(pl.BlockSpec((None, n // LANES, tm, LANES),
                                              lambda i: (i // per_seq, 0, i % per_seq, 0)))
                out_shape.append(jax.ShapeDtypeStruct((m // seq, n // LANES, seq, LANES), o.dtype))
    return pl.pallas_call(
        functools.partial(_norm_proj_kernel, segments=tuple(segments), has_rope=has_rope),
        grid=(m // tm,),
        in_specs=in_specs,
        out_specs=out_specs,
        out_shape=out_shape,
        compiler_params=_params("parallel"),
    )(*args)


def _ffn_kernel(*refs, final_norm, n_side):
    x_ref, g_ref, wg_ref, wu_ref, wd_ref = refs[:5]
    refs = refs[5:]
    if final_norm:
        gf_ref, refs = refs[0], refs[1:]
    side_in, refs = refs[:n_side], refs[n_side:]
    o_ref, refs = refs[0], refs[1:]
    side_out, refs = refs[:n_side], refs[n_side:]
    h_ref, = refs
    j = pl.program_id(1)

    @pl.when(j == 0)
    def _():
        x = x_ref[...]
        h_ref[...] = _rms(x, g_ref[...]).astype(BF16)
        o_ref[...] = x

    for src, dst in zip(side_in, side_out):
        dst[...] = src[...].astype(dst.dtype)

    h = h_ref[...]
    gate = jnp.dot(h, wg_ref[...], preferred_element_type=F32)
    up = jnp.dot(h, wu_ref[...], preferred_element_type=F32)
    act = (gate * jax.nn.sigmoid(gate) * up * 0.5).astype(BF16)
    o_ref[...] += jnp.dot(act, wd_ref[...], preferred_element_type=F32)

    if final_norm:
        @pl.when(j == pl.num_programs(1) - 1)
        def _():
            o_ref[...] = _rms(o_ref[...], gf_ref[...])


def _side_blocks(n_rows, n_steps):
    for nb in range(n_steps, 0, -1):
        if n_rows % nb == 0 and (n_rows // nb) % BF16_SUBLANES == 0:
            return nb
    raise ValueError(f"no aligned row split of {n_rows} rows over {n_steps} steps")


def _ffn(x, g, w_up, w_down, *, tm, tf, final_gain=None, side=()):
    m, d = x.shape
    f = w_down.shape[0]
    nf = f // tf
    assert m % tm == 0 and f % tf == 0 and w_up.shape[1] == 2 * f
    n_steps = (m // tm) * nf
    in_specs = [
        pl.BlockSpec((tm, d), lambda i, j: (i, 0)),
        pl.BlockSpec((1, d), lambda i, j: (0, 0)),
        pl.BlockSpec((d, tf), lambda i, j: (0, j)),
        pl.BlockSpec((d, tf), lambda i, j: (0, nf + j)),
        pl.BlockSpec((tf, d), lambda i, j: (j, 0)),
    ]
    args = [x, g.reshape(1, d), w_up, w_up, w_down]
    if final_gain is not None:
        in_specs.append(pl.BlockSpec((1, d), lambda i, j: (0, 0)))
        args.append(final_gain.reshape(1, d))
    out_specs = [pl.BlockSpec((tm, d), lambda i, j: (i, 0))]
    out_shape = [jax.ShapeDtypeStruct((m, d), F32)]
    for w3, layer in side:
        _, rows, cols = w3.shape
        nb = _side_blocks(rows, n_steps)
        rb = rows // nb
        in_specs.append(pl.BlockSpec((None, rb, cols),
                                     lambda i, j, nb=nb, layer=layer: (layer, jnp.minimum(i * nf + j, nb - 1), 0)))
        out_specs.append(pl.BlockSpec((rb, cols), lambda i, j, nb=nb: (jnp.minimum(i * nf + j, nb - 1), 0)))
        out_shape.append(jax.ShapeDtypeStruct((rows, cols), BF16))
        args.append(w3)
    outs = pl.pallas_call(
        functools.partial(_ffn_kernel, final_norm=final_gain is not None, n_side=len(side)),
        grid=(m // tm, nf),
        in_specs=in_specs,
        out_specs=out_specs,
        out_shape=out_shape,
        scratch_shapes=[pltpu.VMEM((tm, d), BF16)],
        compiler_params=_params("arbitrary", "arbitrary"),
    )(*args)
    return outs[0], list(outs[1:])


def _conv_mixer_kernel(x_ref, g_ref, w_ref, cw_ref, mix_ref, qmem_ref, st_ref, carry_ref, ubuf,
                       *, tiles_per_seq):
    tm = x_ref.shape[0]
    cdim = mix_ref.shape[1]
    chunk = ubuf.shape[1]
    i = pl.program_id(0)

    @pl.when(i % tiles_per_seq == 0)
    def _():
        carry_ref[...] = jnp.zeros(carry_ref.shape, F32)

    h = _rms(x_ref[...], g_ref[...]).astype(BF16)
    cw = cw_ref[...]
    for c0 in range(0, cdim, chunk):
        cs = slice(c0, c0 + chunk)
        b_gate = jnp.dot(h, w_ref[:, c0:c0 + chunk], preferred_element_type=F32)
        c_gate = jnp.dot(h, w_ref[:, cdim + c0:cdim + c0 + chunk], preferred_element_type=F32)
        x_in = jnp.dot(h, w_ref[:, 2 * cdim + c0:2 * cdim + c0 + chunk], preferred_element_type=F32)
        u = c_gate * x_in
        ubuf[0:8, :] = carry_ref[:, cs]
        ubuf[8:tm + 8, :] = u
        conv = cw[0:1, cs] * ubuf[6:tm + 6, :] + cw[1:2, cs] * ubuf[7:tm + 7, :] + cw[2:3, cs] * u
        mix_ref[:, cs] = (b_gate * conv).astype(mix_ref.dtype)
        st_ref[:, cs] = u[tm - (CONV_WIDTH - 1):tm]
        carry_ref[:, cs] = u[tm - 8:tm]
    qmem_ref[...] = jnp.dot(h, w_ref[:, 3 * cdim:], preferred_element_type=F32).astype(qmem_ref.dtype)


def _conv_mixer_prompt(x, g, w, conv_w, *, bsz, tm, cdim):
    m, d = x.shape
    tiles_per_seq = m // bsz // tm
    n_mem_q = w.shape[1] - 3 * cdim
    return pl.pallas_call(
        functools.partial(_conv_mixer_kernel, tiles_per_seq=tiles_per_seq),
        grid=(m // tm,),
        in_specs=[pl.BlockSpec((tm, d), lambda i: (i, 0)),
                  pl.BlockSpec((1, d), lambda i: (0, 0)),
                  pl.BlockSpec(w.shape, lambda i: (0, 0), pipeline_mode=pl.Buffered(1)),
                  pl.BlockSpec((CONV_WIDTH, cdim), lambda i: (0, 0))],
        out_specs=[pl.BlockSpec((tm, cdim), lambda i: (i, 0)),
                   pl.BlockSpec((tm, n_mem_q), lambda i: (i, 0)),
                   pl.BlockSpec((None, CONV_WIDTH - 1, cdim), lambda i: (i // tiles_per_seq, 0, 0))],
        out_shape=[jax.ShapeDtypeStruct((m, cdim), BF16),
                   jax.ShapeDtypeStruct((m, n_mem_q), BF16),
                   jax.ShapeDtypeStruct((bsz, CONV_WIDTH - 1, cdim), F32)],
        scratch_shapes=[pltpu.VMEM((8, cdim), F32), pltpu.VMEM((tm + 8, PROJ_CHUNK), F32)],
        compiler_params=_params("arbitrary"),
    )(x, g.reshape(1, d), w, conv_w)


def _conv_gate_step_kernel(b_ref, c_ref, xin_ref, s0_ref, s1_ref, w_ref, mix_ref, u_ref):
    u = c_ref[...].astype(F32) * xin_ref[...].astype(F32)
    w = w_ref[...]
    conv = w[0:1] * s0_ref[...] + w[1:2] * s1_ref[...] + w[2:3] * u
    mix_ref[...] = (b_ref[...].astype(F32) * conv).astype(mix_ref.dtype)
    u_ref[...] = u


def _conv_gate_step(proj, s0, s1, conv_w, *, cdim):
    m = proj.shape[0]
    col = lambda cb: pl.BlockSpec((m, cdim), lambda i: (0, cb))
    full = lambda r: pl.BlockSpec((r, cdim), lambda i: (0, 0))
    return pl.pallas_call(
        _conv_gate_step_kernel,
        grid=(1,),
        in_specs=[col(0), col(1), col(2), full(m), full(m), full(CONV_WIDTH)],
        out_specs=[full(m), full(m)],
        out_shape=[jax.ShapeDtypeStruct((m, cdim), BF16), jax.ShapeDtypeStruct((m, cdim), F32)],
        compiler_params=_params("arbitrary"),
    )(proj, proj, proj, s0, s1, conv_w)


def _mem_attention(q_ref, k_ref, v_ref):
    rows = q_ref.shape[0]
    q = q_ref[...].astype(BF16)
    if rows < 8:
        q = jnp.broadcast_to(q[0:1], (8, q.shape[1]))
    k = k_ref[...].astype(BF16)
    v = v_ref[...].astype(BF16)
    scale = MEM_HEAD_DIM ** -0.5
    outs = []
    for h in range(MEM_HEADS):
        hs = slice(h * MEM_HEAD_DIM, (h + 1) * MEM_HEAD_DIM)
        s = lax.dot_general(q[:, hs], k[:, hs], (((1,), (1,)), ((), ())),
                            preferred_element_type=F32) * scale
        p = jnp.exp(s - jnp.max(s, axis=-1, keepdims=True))
        l = jnp.sum(p, axis=-1, keepdims=True)
        o = jnp.dot(p.astype(BF16), v[:, hs], preferred_element_type=F32) / l
        outs.append(o[:rows])
    return outs


def _mem_attn_kernel(q_ref, k_ref, v_ref, o_ref):
    for h, o in enumerate(_mem_attention(q_ref, k_ref, v_ref)):
        o_ref[:, h * MEM_HEAD_DIM:(h + 1) * MEM_HEAD_DIM] = o.astype(o_ref.dtype)


def _mem_attn(q3, k3, v3, *, tq):
    nb, rows, width = q3.shape
    n_mem = k3.shape[1]
    return pl.pallas_call(
        _mem_attn_kernel,
        grid=(nb, rows // tq),
        in_specs=[pl.BlockSpec((None, tq, width), lambda b, i: (b, i, 0)),
                  pl.BlockSpec((None, n_mem, width), lambda b, i: (b, 0, 0)),
                  pl.BlockSpec((None, n_mem, width), lambda b, i: (b, 0, 0))],
        out_specs=pl.BlockSpec((None, tq, width), lambda b, i: (b, i, 0)),
        out_shape=jax.ShapeDtypeStruct((nb, rows, width), BF16),
        compiler_params=_params("parallel", "parallel"),
    )(q3, k3, v3)


def _out_proj_kernel(*refs, fused_mem):
    if fused_mem:
        a1_ref, q_ref, k_ref, v_ref, w_ref, x_ref, o_ref = refs
        a2 = jnp.concatenate([o.astype(BF16) for o in _mem_attention(q_ref, k_ref, v_ref)], axis=1)
    else:
        a1_ref, a2_ref, w_ref, x_ref, o_ref = refs
        a2 = a2_ref[...]
    n1 = a1_ref.shape[1]
    acc = jnp.dot(a1_ref[...], w_ref[0:n1, :], preferred_element_type=F32)
    acc = acc + jnp.dot(a2, w_ref[n1:, :], preferred_element_type=F32)
    o_ref[...] = x_ref[...] + acc


def _out_proj(a1, a2, w, x, *, tm, mem=None):
    m, d = x.shape
    n1, n2 = a1.shape[1], a2.shape[1]
    row = lambda n: pl.BlockSpec((tm, n), lambda i: (i, 0))
    in_specs, args = [row(n1), row(n2)], [a1, a2]
    if mem is not None:
        k3, v3, seq = mem
        per_seq = seq // tm
        for kv in (k3, v3):
            in_specs.append(pl.BlockSpec((None,) + kv.shape[1:], lambda i: (i // per_seq, 0, 0)))
            args.append(kv)
    in_specs += [pl.BlockSpec((n1 + n2, d), lambda i: (0, 0), pipeline_mode=pl.Buffered(1)), row(d)]
    args += [w, x]
    return pl.pallas_call(
        functools.partial(_out_proj_kernel, fused_mem=mem is not None),
        grid=(m // tm,),
        in_specs=in_specs,
        out_specs=row(d),
        out_shape=jax.ShapeDtypeStruct((m, d), F32),
        compiler_params=_params("parallel"),
    )(*args)


def _lambda_value(lq1_ref, lk1_ref, lq2_ref, lk2_ref, lam_init):
    a = jnp.exp(jnp.sum(lq1_ref[...] * lk1_ref[...], axis=-1, keepdims=True))
    b = jnp.exp(jnp.sum(lq2_ref[...] * lk2_ref[...], axis=-1, keepdims=True))
    return a - b + lam_init


def _subln(d, gain, lam_init):
    return _rms(d, gain) * (1.0 - lam_init)


def _diff_attn_kernel(q_ref, k_ref, v_ref, lq1_ref, lk1_ref, lq2_ref, lk2_ref, g_ref, o_ref,
                      q2_ref, vt_ref, m_ref, l_ref, acc_ref, *, lam_init):
    tq = q_ref.shape[0]
    tk = vt_ref.shape[2]
    qi = pl.program_id(2)

    @pl.when(qi == 0)
    def _():
        for c in range(vt_ref.shape[0]):
            vt_ref[c] = v_ref[c * tk:(c + 1) * tk, :].T

    q = q_ref[...]
    lane = lax.broadcasted_iota(jnp.int32, q.shape, 1)
    zero = jnp.zeros_like(q)
    q2_ref[0:tq, :] = jnp.where(lane < DIFF_HEAD_DIM, q, zero)
    q2_ref[tq:2 * tq, :] = jnp.where(lane < DIFF_HEAD_DIM, zero, q)
    m_ref[...] = jnp.full(m_ref.shape, -jnp.inf, F32)
    l_ref[...] = jnp.zeros(l_ref.shape, F32)
    acc_ref[...] = jnp.zeros(acc_ref.shape, F32)

    chains = range(0, 2 * tq, ATTN_CHAIN_LANES)

    def n_keys(c, masked):
        return (c % tq) + ATTN_CHAIN_LANES if masked else tk

    def scores(kj, masked):
        start = pl.multiple_of(kj * tk, tk)
        k = k_ref[pl.ds(start, tk), :]
        return [lax.dot_general(k[0:n_keys(c, masked)], q2_ref[c:c + ATTN_CHAIN_LANES, :],
                                (((1,), (1,)), ((), ())), preferred_element_type=F32) for c in chains]

    def absorb(kj, sts, masked):
        vt = vt_ref[kj]
        for c, st in zip(chains, sts):
            cs = slice(c, c + ATTN_CHAIN_LANES)
            if masked:
                key = lax.broadcasted_iota(jnp.int32, st.shape, 0)
                qry = lax.broadcasted_iota(jnp.int32, st.shape, 1) + (c % tq)
                st = jnp.where(key <= qry, st, -jnp.inf)
            m_prev = m_ref[:, cs]
            m_next = jnp.maximum(m_prev, jnp.max(st, axis=0, keepdims=True))
            alpha = jnp.exp2(m_prev - m_next)
            p = jnp.exp2(st - m_next)
            l_ref[:, cs] = alpha * l_ref[:, cs] + jnp.sum(p, axis=0, keepdims=True)
            acc_ref[:, cs] = alpha * acc_ref[:, cs] + jnp.dot(vt[:, 0:n_keys(c, masked)], p.astype(BF16),
                                                              preferred_element_type=F32)
            m_ref[:, cs] = m_next

    def full_blocks(first, count):
        sts = [scores(first + u, False) for u in range(count)]
        for u, st in enumerate(sts):
            absorb(first + u, st, False)

    def quad(t, carry):
        full_blocks(4 * t, 4)
        return carry

    lax.fori_loop(0, qi // 4, quad, 0)

    @pl.when((qi & 2) != 0)
    def _():
        full_blocks((qi // 4) * 4, 2)

    @pl.when((qi & 1) == 0)
    def _():
        absorb(qi, scores(qi, True), True)

    @pl.when((qi & 1) == 1)
    def _():
        sa, sd = scores(qi - 1, False), scores(qi, True)
        absorb(qi - 1, sa, False)
        absorb(qi, sd, True)

    ot = acc_ref[...] / l_ref[...]
    lam = _lambda_value(lq1_ref, lk1_ref, lq2_ref, lk2_ref, lam_init)
    dt = ot[:, 0:tq] - lam * ot[:, tq:2 * tq]
    yt = dt * lax.rsqrt(jnp.mean(dt * dt, axis=0, keepdims=True) + NORM_EPS)
    o_ref[...] = (yt.T * g_ref[...] * (1.0 - lam_init)).astype(o_ref.dtype)


def _diff_attn_prompt(q, k, v4, lam_params, gain, *, bsz, seq, tq, lam_init):
    m = bsz * seq
    width = k.shape[1]
    heads = width // LANES
    nq = seq // tq
    small = lambda a: pl.BlockSpec(a.shape, lambda b, h, i: (0, 0))
    lam_args = [a.reshape(1, -1) for a in lam_params]
    gain = gain.reshape(1, -1)
    return pl.pallas_call(
        functools.partial(_diff_attn_kernel, lam_init=lam_init),
        grid=(bsz, heads, nq),
        in_specs=[pl.BlockSpec((tq, LANES), lambda b, h, i: (b * nq + i, h)),
                  pl.BlockSpec((seq, LANES), lambda b, h, i: (b, h)),
                  pl.BlockSpec((None, None, seq, LANES), lambda b, h, i: (b, h, 0, 0))]
                 + [small(a) for a in lam_args] + [small(gain)],
        out_specs=pl.BlockSpec((tq, LANES), lambda b, h, i: (b * nq + i, h)),
        out_shape=jax.ShapeDtypeStruct((m, width), BF16),
        scratch_shapes=[pltpu.VMEM((2 * tq, LANES), BF16),
                        pltpu.VMEM((seq // tq, LANES, tq), BF16),
                        pltpu.VMEM((1, 2 * tq), F32),
                        pltpu.VMEM((1, 2 * tq), F32),
                        pltpu.VMEM((LANES, 2 * tq), F32)],
        compiler_params=_params("parallel", "parallel", "arbitrary"),
    )(q, k, v4, *lam_args, gain)


def _decode_attn_kernel(pt_ref, q_ref, kn_ref, vn_ref, *refs, n_pages, lam_init):
    del pt_ref
    k_refs, v_refs = refs[:n_pages], refs[n_pages:2 * n_pages]
    lq1_ref, lk1_ref, lq2_ref, lk2_ref, g_ref, o_ref, qbd_ref, m_ref, l_ref, acc_ref = refs[2 * n_pages:]
    g = pl.program_id(1)
    width = q_ref.shape[1]

    @pl.when(g == 0)
    def _():
        row = lax.broadcasted_iota(jnp.int32, (DECODE_ROWS, width), 0)
        lane = lax.broadcasted_iota(jnp.int32, (DECODE_ROWS, width), 1)
        qb = jnp.broadcast_to(q_ref[...], (DECODE_ROWS, width))
        keep = (lane >= row * DIFF_HEAD_DIM) & (lane < (row + 1) * DIFF_HEAD_DIM)
        qbd_ref[...] = jnp.where(keep, qb, 0.0).astype(BF16)
        m_ref[...] = jnp.full(m_ref.shape, -jnp.inf, F32)
        l_ref[...] = jnp.zeros(l_ref.shape, F32)
        acc_ref[...] = jnp.zeros(acc_ref.shape, F32)

    qbd = qbd_ref[...]
    s = jnp.concatenate(
        [jnp.dot(qbd, kr[...].astype(BF16), preferred_element_type=F32) for kr in k_refs], axis=1)
    m_prev = m_ref[:, 0:1]
    m_next = jnp.maximum(m_prev, jnp.max(s, axis=-1, keepdims=True))
    alpha = jnp.exp(m_prev - m_next)
    p = jnp.exp(s - m_next)
    l_next = alpha * l_ref[:, 0:1] + jnp.sum(p, axis=-1, keepdims=True)
    pb = p.astype(BF16)
    page = k_refs[0].shape[1]
    for h in range(width // LANES):
        hs = slice(h * LANES, (h + 1) * LANES)
        pv = jnp.dot(pb[:, 0:page], v_refs[0][h].astype(BF16), preferred_element_type=F32)
        for r in range(1, n_pages):
            pv = pv + jnp.dot(pb[:, r * page:(r + 1) * page], v_refs[r][h].astype(BF16),
                              preferred_element_type=F32)
        acc_ref[:, hs] = alpha * acc_ref[:, hs] + pv
    m_ref[...] = jnp.broadcast_to(m_next, m_ref.shape)
    l_ref[...] = jnp.broadcast_to(l_next, l_ref.shape)

    @pl.when(g == pl.num_programs(1) - 1)
    def _():
        kn = kn_ref[...].astype(BF16).astype(F32)
        s_new = jnp.sum(qbd_ref[...].astype(F32) * kn, axis=-1, keepdims=True)
        m_old = m_ref[:, 0:1]
        m_fin = jnp.maximum(m_old, s_new)
        a = jnp.exp(m_old - m_fin)
        p_new = jnp.exp(s_new - m_fin)
        l_fin = a * l_ref[:, 0:1] + p_new
        o = (a * acc_ref[...] + p_new * vn_ref[...]) / l_fin
        lam = _lambda_value(lq1_ref, lk1_ref, lq2_ref, lk2_ref, lam_init)
        gain = g_ref[...]
        for h in range(width // LANES):
            hs = slice(h * LANES, (h + 1) * LANES)
            d = o[2 * h:2 * h + 1, hs] - lam * o[2 * h + 1:2 * h + 2, hs]
            o_ref[:, hs] = _subln(d, gain, lam_init).astype(o_ref.dtype)


def _decode_attn(q, k_new, v_new, cache_k, cache_v, page_table, lam_params, gain, *, lam_init):
    nb, _, width = q.shape
    page = cache_k.shape[2]
    n_used = page_table.shape[1]
    npg = DECODE_PAGES_PER_STEP
    assert n_used % npg == 0
    row = pl.BlockSpec((None, 1, width), lambda b, g, pt: (b, 0, 0))
    small = lambda a: pl.BlockSpec(a.shape, lambda b, g, pt: (0, 0))
    k_spec = lambda r: pl.BlockSpec((None, width, page),
                                    lambda b, g, pt: (pt[b, g * npg + r], 0, 0))
    v_spec = lambda r: pl.BlockSpec((None, width // LANES, page, LANES),
                                    lambda b, g, pt: (pt[b, g * npg + r], 0, 0, 0))
    lam_args = [a.reshape(1, -1) for a in lam_params]
    gain = gain.reshape(1, -1)
    grid_spec = pltpu.PrefetchScalarGridSpec(
        num_scalar_prefetch=1,
        grid=(nb, n_used // npg),
        in_specs=[row, row, row] + [k_spec(r) for r in range(npg)] + [v_spec(r) for r in range(npg)]
                 + [small(a) for a in lam_args] + [small(gain)],
        out_specs=pl.BlockSpec((None, 1, width), lambda b, g, pt: (b, 0, 0)),
        scratch_shapes=[pltpu.VMEM((DECODE_ROWS, width), BF16),
                        pltpu.VMEM((DECODE_ROWS, LANES), F32),
                        pltpu.VMEM((DECODE_ROWS, LANES), F32),
                        pltpu.VMEM((DECODE_ROWS, width), F32)],
    )
    return pl.pallas_call(
        functools.partial(_decode_attn_kernel, n_pages=npg, lam_init=lam_init),
        grid_spec=grid_spec,
        out_shape=jax.ShapeDtypeStruct((nb, 1, width), BF16),
        compiler_params=_params("parallel", "arbitrary"),
    )(page_table, q, k_new, v_new, *([cache_k] * npg), *([cache_v] * npg), *lam_args, gain)


def _rope_tables(pos):
    half = DIFF_HEAD_DIM // 2
    inv_freq = ROPE_THETA ** (-jnp.arange(half, dtype=F32) / half)
    ang = pos.astype(F32)[:, None] * inv_freq[None, :]
    cos, sin, zero = jnp.cos(ang), jnp.sin(ang), jnp.zeros_like(ang)
    reps = LANES // DIFF_HEAD_DIM
    cos_t = jnp.tile(jnp.concatenate([cos, cos], axis=1), (1, reps))
    sin_lo = jnp.tile(jnp.concatenate([-sin, zero], axis=1), (1, reps))
    sin_hi = jnp.tile(jnp.concatenate([zero, sin], axis=1), (1, reps))
    return cos_t, sin_lo, sin_hi


def _trunk(x, p, wb, side_plan, *, bsz, seq, tiles, rope, mem_k, mem_v, conv_state, cache, depth):
    m, d = x.shape
    n_a = depth // 2
    prompt = cache is None
    cdim = p['conv_w'].shape[-1]
    qk_width = p['qk_width']
    mem_width = MEM_HEADS * MEM_HEAD_DIM
    tm = tiles.proj
    act = BF16 if prompt else F32
    new_conv = []
    k_out = v_out = k_b = v_b = None

    def ffn(name, l, x, final_gain=None):
        jobs = side_plan.get((name, l), [])
        y, copies = _ffn(x, p['norm_' + name][l], wb[('w_' + name + '_up', l)], wb[('w_' + name + '_down', l)],
                         tm=tiles.ffn, tf=tiles.ffn_cols, final_gain=final_gain,
                         side=[(w3, layer) for _, w3, layer in jobs])
        for (key, _, _), copy in zip(jobs, copies):
            wb[key] = copy
        return y

    for l in range(depth):
        if l == n_a:
            if prompt:
                k_out, k_b, v_out, v_b = _norm_proj(
                    x, p['norm_kv'], wb[('w_kv', 0)],
                    [_Segment(qk_width, (_Out(F32, 'cols'), _Out(BF16)), rope=True),
                     _Segment(qk_width, (_Out(F32, 'heads'), _Out(BF16, 'heads')))],
                    tm=tm, seq=seq, rope=rope)
            else:
                k_out, v_out = _norm_proj(
                    x, p['norm_kv'], wb[('w_kv', 0)],
                    [_Segment(qk_width, (_Out(F32),), rope=True), _Segment(qk_width, (_Out(F32),))],
                    tm=tm, seq=seq, rope=rope)
        x = ffn('ffn1', l, x)
        if l < n_a:
            if prompt:
                mix, q_mem, st = _conv_mixer_prompt(x, p['norm_mix'][l], wb[('w_in_a', l)], p['conv_w'][l],
                                                    bsz=bsz, tm=tm, cdim=cdim)
            else:
                proj, q_mem = _norm_proj(x, p['norm_mix'][l], wb[('w_in_a', l)],
                                         [_Segment(3 * cdim, (_Out(act),)), _Segment(mem_width, (_Out(act),))],
                                         tm=tm, seq=seq)
                s0, s1 = conv_state[l]
                mix, u = _conv_gate_step(proj, s0, s1, p['conv_w'][l], cdim=cdim)
                st = jnp.stack([s1, u], axis=1)
            new_conv.append(st)
        else:
            j = l - n_a
            lam_init = 0.8 - 0.6 * math.exp(-0.3 * l)
            lam_params = (p['lambda_q1'][j], p['lambda_k1'][j], p['lambda_q2'][j], p['lambda_k2'][j])
            q_scale = DIFF_HEAD_DIM ** -0.5 * (LOG2E if prompt else 1.0)
            q, q_mem = _norm_proj(
                x, p['norm_mix'][l], wb[('w_in_b', j)],
                [_Segment(qk_width, (_Out(act),), rope=True, scale=q_scale),
                 _Segment(mem_width, (_Out(act),))],
                tm=tm, seq=seq, rope=rope)
            if prompt:
                mix = _diff_attn_prompt(q, k_b, v_b, lam_params, p['subln_gain'][j],
                                        bsz=bsz, seq=seq, tq=tiles.attn, lam_init=lam_init)
            else:
                cache_k, cache_v, page_table = cache
                mix = _decode_attn(q.reshape(bsz, 1, qk_width),
                                   k_out.reshape(bsz, 1, qk_width), v_out.reshape(bsz, 1, qk_width),
                                   cache_k, cache_v, page_table, lam_params, p['subln_gain'][j],
                                   lam_init=lam_init).reshape(m, qk_width)
        if prompt:
            x = _out_proj(mix, q_mem, wb[('w_out', l)], x, tm=tm, mem=(mem_k[l], mem_v[l], seq))
        else:
            mem_o = _mem_attn(q_mem.reshape(bsz, seq, mem_width), mem_k[l], mem_v[l], tq=seq)
            x = _out_proj(mix, mem_o.reshape(m, mem_width), wb[('w_out', l)], x, tm=tm)
        x = ffn('ffn2', l, x, final_gain=p['norm_final'] if l == depth - 1 else None)
    return x, new_conv, k_out, v_out


def kernel(x_prompt, x_sample, state_conv, cache_k, cache_v, cache_mem_k, cache_mem_v, page_table, mem_prompt, norm_ffn1, w_ffn1_up, w_ffn1_down, norm_mix, w_in_a, conv_w, w_in_b, lambda_q1, lambda_k1, lambda_q2, lambda_k2, subln_gain, norm_mem, w_mem_kv, w_out, norm_ffn2, w_ffn2_up, w_ffn2_down, norm_kv, w_kv, norm_final):
    bsz_p, s_p, d = x_prompt.shape
    bsz_s, s_s, _ = x_sample.shape
    depth = norm_ffn1.shape[0]
    n_mem = mem_prompt.shape[1]
    mem_width = MEM_HEADS * MEM_HEAD_DIM
    n_pool, page = cache_k.shape[0], cache_k.shape[1]
    k_heads, v_heads = cache_k.shape[2], cache_v.shape[2]
    qk_width = k_heads * cache_k.shape[3]
    past_len = page_table.shape[1] * page
    assert s_s == 1

    n_a = depth // 2
    p = {
        'norm_ffn1': norm_ffn1, 'norm_ffn2': norm_ffn2, 'norm_mix': norm_mix, 'conv_w': conv_w,
        'lambda_q1': lambda_q1, 'lambda_k1': lambda_k1, 'lambda_q2': lambda_q2, 'lambda_k2': lambda_k2,
        'subln_gain': subln_gain, 'norm_kv': norm_kv, 'norm_final': norm_final, 'qk_width': qk_width,
    }
    stacked = {'w_ffn1_up': w_ffn1_up, 'w_ffn1_down': w_ffn1_down, 'w_ffn2_up': w_ffn2_up,
               'w_ffn2_down': w_ffn2_down, 'w_in_a': w_in_a, 'w_in_b': w_in_b, 'w_out': w_out,
               'w_kv': w_kv[None]}

    def job(name, layer):
        return ((name, layer), stacked[name], layer)

    first = [job('w_ffn1_up', 0), job('w_ffn1_down', 0)] + ([job('w_kv', 0)] if n_a == 0 else [])
    wb = {key: _cast_layer(w3, layer) for key, w3, layer in first}
    side_plan = {}
    for l in range(depth):
        mixer = job('w_in_a', l) if l < n_a else job('w_in_b', l - n_a)
        side_plan[('ffn1', l)] = [mixer, job('w_out', l), job('w_ffn2_up', l), job('w_ffn2_down', l)]
        if l + 1 < depth:
            shared_kv = [job('w_kv', 0)] if l + 1 == n_a else []
            side_plan[('ffn2', l)] = shared_kv + [job('w_ffn1_up', l + 1), job('w_ffn1_down', l + 1)]

    mem_rows = mem_prompt.reshape(bsz_p * n_mem, d)
    mem_kv = [_norm_proj(mem_rows, norm_mem[l], w_mem_kv,
                         [_Segment(mem_width, (_Out(F32),)), _Segment(mem_width, (_Out(F32),))],
                         tm=bsz_p * n_mem, seq=n_mem, layer=l)
              for l in range(depth)]
    mem_k_p = [kv[0].reshape(bsz_p, n_mem, mem_width) for kv in mem_kv]
    mem_v_p = [kv[1].reshape(bsz_p, n_mem, mem_width) for kv in mem_kv]
    y_p, conv_p, k_t, v_h = _trunk(
        x_prompt.reshape(bsz_p * s_p, d), p, wb, side_plan, bsz=bsz_p, seq=s_p, tiles=PROMPT_TILES,
        rope=_rope_tables(jnp.arange(s_p)), mem_k=mem_k_p, mem_v=mem_v_p,
        conv_state=None, cache=None, depth=depth)

    rows_s = bsz_s * s_s
    pos_s = jnp.full((rows_s,), past_len, jnp.int32)
    cmk = cache_mem_k.reshape(depth, bsz_s, n_mem, mem_width)
    cmv = cache_mem_v.reshape(depth, bsz_s, n_mem, mem_width)
    y_s, conv_s, k_s, v_s = _trunk(
        x_sample.reshape(rows_s, d), p, wb, {}, bsz=bsz_s, seq=s_s,
        tiles=_Tiles(proj=rows_s, ffn=rows_s, ffn_cols=SAMPLE_FFN_COLS, attn=0),
        rope=_rope_tables(pos_s),
        mem_k=[cmk[l] for l in range(depth)], mem_v=[cmv[l] for l in range(depth)],
        conv_state=[(state_conv[l, :, 0], state_conv[l, :, 1]) for l in range(n_a)],
        cache=(jnp.transpose(cache_k, (0, 2, 3, 1)).reshape(n_pool, qk_width, page),
               jnp.transpose(cache_v, (0, 2, 1, 3)), page_table),
        depth=depth)

    mem_shape = (depth, bsz_p, n_mem, MEM_HEADS, MEM_HEAD_DIM)
    return (y_p.reshape(bsz_p, s_p, d),
            y_s.reshape(bsz_s, s_s, d),
            jnp.stack(conv_p, axis=0),
            jnp.stack(conv_s, axis=0),
            jnp.transpose(k_t.reshape(bsz_p, k_heads, -1, s_p), (0, 3, 1, 2)),
            jnp.transpose(v_h, (0, 2, 1, 3)),
            k_s.reshape(bsz_s, s_s, k_heads, -1),
            v_s.reshape(bsz_s, s_s, v_heads, -1),
            jnp.stack(mem_k_p, axis=0).reshape(mem_shape),
            jnp.stack(mem_v_p, axis=0).reshape(mem_shape))
```

```python
import functools
import math
from typing import NamedTuple

import jax
import jax.numpy as jnp
from jax import lax
from jax.experimental import pallas as pl
from jax.experimental.pallas import tpu as pltpu

F32 = jnp.float32
BF16 = jnp.bfloat16

NORM_EPS = 1e-6
ROPE_THETA = 10000.0
LOG2E = math.log2(math.e)
LANES = 128
BF16_SUBLANES = 16
VMEM_LIMIT = 56 * 1024 * 1024
PROJ_CHUNK = 512
CAST_BLOCK_BYTES = 8 * 1024 * 1024

MEM_HEADS = 4
MEM_HEAD_DIM = 128
DIFF_HEAD_DIM = 64
ROPE_HALF = DIFF_HEAD_DIM // 2
CONV_WIDTH = 3
ATTN_CHAIN_LANES = 256
DECODE_PAGES_PER_STEP = 8
DECODE_ROWS = 32


class _Tiles(NamedTuple):
    proj: int
    ffn: int
    ffn_cols: int
    attn: int


PROMPT_TILES = _Tiles(proj=512, ffn=512, ffn_cols=512, attn=512)


def _params(*sem):
    return pltpu.CompilerParams(dimension_semantics=sem, vmem_limit_bytes=VMEM_LIMIT)


def _rms(x, g):
    ms = jnp.mean(x * x, axis=-1, keepdims=True)
    return x * lax.rsqrt(ms + NORM_EPS) * g


def _cast_kernel(w_ref, o_ref):
    o_ref[...] = w_ref[...].astype(o_ref.dtype)


def _cast_layer(w3, layer):
    _, k, n = w3.shape
    rows = k
    while rows * n * 4 > CAST_BLOCK_BYTES and rows % 16 == 0:
        rows //= 2
    return pl.pallas_call(
        _cast_kernel,
        grid=(k // rows,),
        in_specs=[pl.BlockSpec((None, rows, n), lambda i: (layer, i, 0))],
        out_specs=pl.BlockSpec((rows, n), lambda i: (i, 0)),
        out_shape=jax.ShapeDtypeStruct((k, n), BF16),
        compiler_params=_params("parallel"),
    )(w3)


def _rope_tile(y, cos, sin_lo, sin_hi):
    outs = []
    for c in range(y.shape[1] // LANES):
        yc = y[:, c * LANES:(c + 1) * LANES]
        up = pltpu.roll(yc, LANES - ROPE_HALF, axis=1)
        dn = pltpu.roll(yc, ROPE_HALF, axis=1)
        outs.append(yc * cos + up * sin_lo + dn * sin_hi)
    return jnp.concatenate(outs, axis=1) if len(outs) > 1 else outs[0]


class _Out(NamedTuple):
    dtype: object
    layout: str = 'rows'


class _Segment(NamedTuple):
    n_cols: int
    outs: tuple
    rope: bool = False
    scale: float = 1.0


def _norm_proj_kernel(*refs, segments, has_rope):
    if has_rope:
        x_ref, g_ref, w_ref, cos_ref, slo_ref, shi_ref = refs[:6]
        out_refs = refs[6:]
    else:
        x_ref, g_ref, w_ref = refs[:3]
        out_refs = refs[3:]
    h = _rms(x_ref[...], g_ref[...]).astype(BF16)
    col, k = 0, 0
    for seg in segments:
        outs = out_refs[k:k + len(seg.outs)]
        for c0 in range(0, seg.n_cols, PROJ_CHUNK):
            c1 = min(c0 + PROJ_CHUNK, seg.n_cols)
            y = jnp.dot(h, w_ref[:, col + c0:col + c1].astype(BF16), preferred_element_type=F32)
            if seg.rope:
                y = _rope_tile(y, cos_ref[...], slo_ref[...], shi_ref[...]) * seg.scale
            for o, spec in zip(outs, seg.outs):
                if spec.layout == 'rows':
                    o[:, c0:c1] = y.astype(o.dtype)
                elif spec.layout == 'cols':
                    o[c0:c1, :] = y.T.astype(o.dtype)
                else:
                    for hh in range(c0 // LANES, c1 // LANES):
                        o[hh] = y[:, hh * LANES - c0:(hh + 1) * LANES - c0].astype(o.dtype)
        col += seg.n_cols
        k += len(seg.outs)


def _norm_proj(x, g, w, segments, *, tm, seq, rope=None, layer=None):
    m, d = x.shape
    assert m % tm == 0 and sum(s.n_cols for s in segments) == w.shape[-1]
    has_rope = any(s.rope for s in segments)
    row_major = all(o.layout == 'rows' for s in segments for o in s.outs)
    assert row_major or seq % tm == 0
    per_seq = max(seq // tm, 1)
    if layer is None:
        w_spec = pl.BlockSpec(w.shape, lambda i: (0, 0), pipeline_mode=pl.Buffered(1))
    else:
        w_spec = pl.BlockSpec((None,) + w.shape[1:], lambda i: (layer, 0, 0), pipeline_mode=pl.Buffered(1))
    in_specs = [
        pl.BlockSpec((tm, d), lambda i: (i, 0)),
        pl.BlockSpec((1, d), lambda i: (0, 0)),
        w_spec,
    ]
    args = [x, g.reshape(1, d), w]
    if has_rope:
        n_tab = rope[0].shape[0] // tm
        for t in rope:
            in_specs.append(pl.BlockSpec((tm, LANES), lambda i: (i % n_tab, 0)))
            args.append(t)
    out_specs, out_shape = [], []
    for seg in segments:
        n = seg.n_cols
        for o in seg.outs:
            if o.layout == 'rows':
                out_specs.append(pl.BlockSpec((tm, n), lambda i: (i, 0)))
                out_shape.append(jax.ShapeDtypeStruct((m, n), o.dtype))
            elif o.layout == 'cols':
                out_specs.append(pl.BlockSpec((None, n, tm), lambda i: (i // per_seq, 0, i % per_seq)))
                out_shape.append(jax.ShapeDtypeStruct((m // seq, n, seq), o.dtype))
            else:
                out_specs.append(pl.BlockSpec((None, n // LANES, tm, LANES),
                                              lambda i: (i // per_seq, 0, i % per_seq, 0)))
                out_shape.append(jax.ShapeDtypeStruct((m // seq, n // LANES, seq, LANES), o.dtype))
    return pl.pallas_call(
        functools.partial(_norm_proj_kernel, segments=tuple(segments), has_rope=has_rope),
        grid=(m // tm,),
        in_specs=in_specs,
        out_specs=out_specs,
        out_shape=out_shape,
        compiler_params=_params("parallel"),
    )(*args)


def _ffn_kernel(*refs, final_norm, n_side):
    x_ref, g_ref, wg_ref, wu_ref, wd_ref = refs[:5]
    refs = refs[5:]
    if final_norm:
        gf_ref, refs = refs[0], refs[1:]
    side_in, refs = refs[:n_side], refs[n_side:]
    o_ref, refs = refs[0], refs[1:]
    side_out, refs = refs[:n_side], refs[n_side:]
    h_ref, = refs
    j = pl.program_id(1)

    @pl.when(j == 0)
    def _():
        x = x_ref[...]
        h_ref[...] = _rms(x, g_ref[...]).astype(BF16)
        o_ref[...] = x

    for src, dst in zip(side_in, side_out):
        dst[...] = src[...].astype(dst.dtype)

    h = h_ref[...]
    gate = jnp.dot(h, wg_ref[...], preferred_element_type=F32)
    up = jnp.dot(h, wu_ref[...], preferred_element_type=F32)
    act = (gate * jax.nn.sigmoid(gate) * up * 0.5).astype(BF16)
    o_ref[...] += jnp.dot(act, wd_ref[...], preferred_element_type=F32)

    if final_norm:
        @pl.when(j == pl.num_programs(1) - 1)
        def _():
            o_ref[...] = _rms(o_ref[...], gf_ref[...])


def _side_blocks(n_rows, n_steps):
    for nb in range(n_steps, 0, -1):
        if n_rows % nb == 0 and (n_rows // nb) % BF16_SUBLANES == 0:
            return nb
    raise ValueError(f"no aligned row split of {n_rows} rows over {n_steps} steps")


def _ffn(x, g, w_up, w_down, *, tm, tf, final_gain=None, side=()):
    m, d = x.shape
    f = w_down.shape[0]
    nf = f // tf
    assert m % tm == 0 and f % tf == 0 and w_up.shape[1] == 2 * f
    n_steps = (m // tm) * nf
    in_specs = [
        pl.BlockSpec((tm, d), lambda i, j: (i, 0)),
        pl.BlockSpec((1, d), lambda i, j: (0, 0)),
        pl.BlockSpec((d, tf), lambda i, j: (0, j)),
        pl.BlockSpec((d, tf), lambda i, j: (0, nf + j)),
        pl.BlockSpec((tf, d), lambda i, j: (j, 0)),
    ]
    args = [x, g.reshape(1, d), w_up, w_up, w_down]
    if final_gain is not None:
        in_specs.append(pl.BlockSpec((1, d), lambda i, j: (0, 0)))
        args.append(final_gain.reshape(1, d))
    out_specs = [pl.BlockSpec((tm, d), lambda i, j: (i, 0))]
    out_shape = [jax.ShapeDtypeStruct((m, d), F32)]
    for w3, layer in side:
        _, rows, cols = w3.shape
        nb = _side_blocks(rows, n_steps)
        rb = rows // nb
        in_specs.append(pl.BlockSpec((None, rb, cols),
                                     lambda i, j, nb=nb, layer=layer: (layer, jnp.minimum(i * nf + j, nb - 1), 0)))
        out_specs.append(pl.BlockSpec((rb, cols), lambda i, j, nb=nb: (jnp.minimum(i * nf + j, nb - 1), 0)))
        out_shape.append(jax.ShapeDtypeStruct((rows, cols), BF16))
        args.append(w3)
    outs = pl.pallas_call(
        functools.partial(_ffn_kernel, final_norm=final_gain is not None, n_side=len(side)),
        grid=(m // tm, nf),
        in_specs=in_specs,
        out_specs=out_specs,
        out_shape=out_shape,
        scratch_shapes=[pltpu.VMEM((tm, d), BF16)],
        compiler_params=_params("arbitrary", "arbitrary"),
    )(*args)
    return outs[0], list(outs[1:])


def _conv_mixer_kernel(x_ref, g_ref, w_ref, cw_ref, mix_ref, qmem_ref, st_ref, carry_ref, ubuf,
                       *, tiles_per_seq):
    tm = x_ref.shape[0]
    cdim = mix_ref.shape[1]
    chunk = ubuf.shape[1]
    i = pl.program_id(0)

    @pl.when(i % tiles_per_seq == 0)
    def _():
        carry_ref[...] = jnp.zeros(carry_ref.shape, F32)

    h = _rms(x_ref[...], g_ref[...]).astype(BF16)
    cw = cw_ref[...]
    for c0 in range(0, cdim, chunk):
        cs = slice(c0, c0 + chunk)
        b_gate = jnp.dot(h, w_ref[:, c0:c0 + chunk], preferred_element_type=F32)
        c_gate = jnp.dot(h, w_ref[:, cdim + c0:cdim + c0 + chunk], preferred_element_type=F32)
        x_in = jnp.dot(h, w_ref[:, 2 * cdim + c0:2 * cdim + c0 + chunk], preferred_element_type=F32)
        u = c_gate * x_in
        ubuf[0:8, :] = carry_ref[:, cs]
        ubuf[8:tm + 8, :] = u
        conv = cw[0:1, cs] * ubuf[6:tm + 6, :] + cw[1:2, cs] * ubuf[7:tm + 7, :] + cw[2:3, cs] * u
        mix_ref[:, cs] = (b_gate * conv).astype(mix_ref.dtype)
        st_ref[:, cs] = u[tm - (CONV_WIDTH - 1):tm]
        carry_ref[:, cs] = u[tm - 8:tm]
    qmem_ref[...] = jnp.dot(h, w_ref[:, 3 * cdim:], preferred_element_type=F32).astype(qmem_ref.dtype)


def _conv_mixer_prompt(x, g, w, conv_w, *, bsz, tm, cdim):
    m, d = x.shape
    tiles_per_seq = m // bsz // tm
    n_mem_q = w.shape[1] - 3 * cdim
    return pl.pallas_call(
        functools.partial(_conv_mixer_kernel, tiles_per_seq=tiles_per_seq),
        grid=(m // tm,),
        in_specs=[pl.BlockSpec((tm, d), lambda i: (i, 0)),
                  pl.BlockSpec((1, d), lambda i: (0, 0)),
                  pl.BlockSpec(w.shape, lambda i: (0, 0), pipeline_mode=pl.Buffered(1)),
                  pl.BlockSpec((CONV_WIDTH, cdim), lambda i: (0, 0))],
        out_specs=[pl.BlockSpec((tm, cdim), lambda i: (i, 0)),
                   pl.BlockSpec((tm, n_mem_q), lambda i: (i, 0)),
                   pl.BlockSpec((None, CONV_WIDTH - 1, cdim), lambda i: (i // tiles_per_seq, 0, 0))],
        out_shape=[jax.ShapeDtypeStruct((m, cdim), BF16),
                   jax.ShapeDtypeStruct((m, n_mem_q), BF16),
                   jax.ShapeDtypeStruct((bsz, CONV_WIDTH - 1, cdim), F32)],
        scratch_shapes=[pltpu.VMEM((8, cdim), F32), pltpu.VMEM((tm + 8, PROJ_CHUNK), F32)],
        compiler_params=_params("arbitrary"),
    )(x, g.reshape(1, d), w, conv_w)


def _conv_gate_step_kernel(b_ref, c_ref, xin_ref, s0_ref, s1_ref, w_ref, mix_ref, u_ref):
    u = c_ref[...].astype(F32) * xin_ref[...].astype(F32)
    w = w_ref[...]
    conv = w[0:1] * s0_ref[...] + w[1:2] * s1_ref[...] + w[2:3] * u
    mix_ref[...] = (b_ref[...].astype(F32) * conv).astype(mix_ref.dtype)
    u_ref[...] = u


def _conv_gate_step(proj, s0, s1, conv_w, *, cdim):
    m = proj.shape[0]
    col = lambda cb: pl.BlockSpec((m, cdim), lambda i: (0, cb))
    full = lambda r: pl.BlockSpec((r, cdim), lambda i: (0, 0))
    return pl.pallas_call(
        _conv_gate_step_kernel,
        grid=(1,),
        in_specs=[col(0), col(1), col(2), full(m), full(m), full(CONV_WIDTH)],
        out_specs=[full(m), full(m)],
        out_shape=[jax.ShapeDtypeStruct((m, cdim), BF16), jax.ShapeDtypeStruct((m, cdim), F32)],
        compiler_params=_params("arbitrary"),
    )(proj, proj, proj, s0, s1, conv_w)


def _mem_attention(q_ref, k_ref, v_ref):
    rows = q_ref.shape[0]
    q = q_ref[...].astype(BF16)
    if rows < 8:
        q = jnp.broadcast_to(q[0:1], (8, q.shape[1]))
    k = k_ref[...].astype(BF16)
    v = v_ref[...].astype(BF16)
    scale = MEM_HEAD_DIM ** -0.5
    outs = []
    for h in range(MEM_HEADS):
        hs = slice(h * MEM_HEAD_DIM, (h + 1) * MEM_HEAD_DIM)
        s = lax.dot_general(q[:, hs], k[:, hs], (((1,), (1,)), ((), ())),
                            preferred_element_type=F32) * scale
        p = jnp.exp(s - jnp.max(s, axis=-1, keepdims=True))
        l = jnp.sum(p, axis=-1, keepdims=True)
        o = jnp.dot(p.astype(BF16), v[:, hs], preferred_element_type=F32) / l
        outs.append(o[:rows])
    return outs


def _mem_attn_kernel(q_ref, k_ref, v_ref, o_ref):
    for h, o in enumerate(_mem_attention(q_ref, k_ref, v_ref)):
        o_ref[:, h * MEM_HEAD_DIM:(h + 1) * MEM_HEAD_DIM] = o.astype(o_ref.dtype)


def _mem_attn(q3, k3, v3, *, tq):
    nb, rows, width = q3.shape
    n_mem = k3.shape[1]
    return pl.pallas_call(
        _mem_attn_kernel,
        grid=(nb, rows // tq),
        in_specs=[pl.BlockSpec((None, tq, width), lambda b, i: (b, i, 0)),
                  pl.BlockSpec((None, n_mem, width), lambda b, i: (b, 0, 0)),
                  pl.BlockSpec((None, n_mem, width), lambda b, i: (b, 0, 0))],
        out_specs=pl.BlockSpec((None, tq, width), lambda b, i: (b, i, 0)),
        out_shape=jax.ShapeDtypeStruct((nb, rows, width), BF16),
        compiler_params=_params("parallel", "parallel"),
    )(q3, k3, v3)


def _out_proj_kernel(*refs, fused_mem):
    if fused_mem:
        a1_ref, q_ref, k_ref, v_ref, w_ref, x_ref, o_ref = refs
        a2 = jnp.concatenate([o.astype(BF16) for o in _mem_attention(q_ref, k_ref, v_ref)], axis=1)
    else:
        a1_ref, a2_ref, w_ref, x_ref, o_ref = refs
        a2 = a2_ref[...]
    n1 = a1_ref.shape[1]
    acc = jnp.dot(a1_ref[...], w_ref[0:n1, :], preferred_element_type=F32)
    acc = acc + jnp.dot(a2, w_ref[n1:, :], preferred_element_type=F32)
    o_ref[...] = x_ref[...] + acc


def _out_proj(a1, a2, w, x, *, tm, mem=None):
    m, d = x.shape
    n1, n2 = a1.shape[1], a2.shape[1]
    row = lambda n: pl.BlockSpec((tm, n), lambda i: (i, 0))
    in_specs, args = [row(n1), row(n2)], [a1, a2]
    if mem is not None:
        k3, v3, seq = mem
        per_seq = seq // tm
        for kv in (k3, v3):
            in_specs.append(pl.BlockSpec((None,) + kv.shape[1:], lambda i: (i // per_seq, 0, 0)))
            args.append(kv)
    in_specs += [pl.BlockSpec((n1 + n2, d), lambda i: (0, 0), pipeline_mode=pl.Buffered(1)), row(d)]
    args += [w, x]
    return pl.pallas_call(
        functools.partial(_out_proj_kernel, fused_mem=mem is not None),
        grid=(m // tm,),
        in_specs=in_specs,
        out_specs=row(d),
        out_shape=jax.ShapeDtypeStruct((m, d), F32),
        compiler_params=_params("parallel"),
    )(*args)


def _lambda_value(lq1_ref, lk1_ref, lq2_ref, lk2_ref, lam_init):
    a = jnp.exp(jnp.sum(lq1_ref[...] * lk1_ref[...], axis=-1, keepdims=True))
    b = jnp.exp(jnp.sum(lq2_ref[...] * lk2_ref[...], axis=-1, keepdims=True))
    return a - b + lam_init


def _subln(d, gain, lam_init):
    return _rms(d, gain) * (1.0 - lam_init)


def _diff_attn_kernel(q_ref, k_ref, v_ref, lq1_ref, lk1_ref, lq2_ref, lk2_ref, g_ref, o_ref,
                      q2_ref, vt_ref, m_ref, l_ref, acc_ref, *, lam_init):
    tq = q_ref.shape[0]
    tk = vt_ref.shape[2]
    qi = pl.program_id(2)

    @pl.when(qi == 0)
    def _():
        for c in range(vt_ref.shape[0]):
            vt_ref[c] = v_ref[c * tk:(c + 1) * tk, :].T

    q = q_ref[...]
    lane = lax.broadcasted_iota(jnp.int32, q.shape, 1)
    zero = jnp.zeros_like(q)
    q2_ref[0:tq, :] = jnp.where(lane < DIFF_HEAD_DIM, q, zero)
    q2_ref[tq:2 * tq, :] = jnp.where(lane < DIFF_HEAD_DIM, zero, q)
    m_ref[...] = jnp.full(m_ref.shape, -jnp.inf, F32)
    l_ref[...] = jnp.zeros(l_ref.shape, F32)
    acc_ref[...] = jnp.zeros(acc_ref.shape, F32)

    chains = range(0, 2 * tq, ATTN_CHAIN_LANES)

    def n_keys(c, masked):
        return (c % tq) + ATTN_CHAIN_LANES if masked else tk

    def scores(kj, masked):
        start = pl.multiple_of(kj * tk, tk)
        k = k_ref[pl.ds(start, tk), :]
        return [lax.dot_general(k[0:n_keys(c, masked)], q2_ref[c:c + ATTN_CHAIN_LANES, :],
                                (((1,), (1,)), ((), ())), preferred_element_type=F32) for c in chains]

    def absorb(kj, sts, masked):
        vt = vt_ref[kj]
        for c, st in zip(chains, sts):
            cs = slice(c, c + ATTN_CHAIN_LANES)
            if masked:
                key = lax.broadcasted_iota(jnp.int32, st.shape, 0)
                qry = lax.broadcasted_iota(jnp.int32, st.shape, 1) + (c % tq)
                st = jnp.where(key <= qry, st, -jnp.inf)
            m_prev = m_ref[:, cs]
            m_next = jnp.maximum(m_prev, jnp.max(st, axis=0, keepdims=True))
            alpha = jnp.exp2(m_prev - m_next)
            p = jnp.exp2(st - m_next)
            l_ref[:, cs] = alpha * l_ref[:, cs] + jnp.sum(p, axis=0, keepdims=True)
            acc_ref[:, cs] = alpha * acc_ref[:, cs] + jnp.dot(vt[:, 0:n_keys(c, masked)], p.astype(BF16),
                                                              preferred_element_type=F32)
            m_ref[:, cs] = m_next

    def full_blocks(first, count):
        sts = [scores(first + u, False) for u in range(count)]
        for u, st in enumerate(sts):
            absorb(first + u, st, False)

    def quad(t, carry):
        full_blocks(4 * t, 4)
        return carry

    lax.fori_loop(0, qi // 4, quad, 0)

    @pl.when((qi & 2) != 0)
    def _():
        full_blocks((qi // 4) * 4, 2)

    @pl.when((qi & 1) == 0)
    def _():
        absorb(qi, scores(qi, True), True)

    @pl.when((qi & 1) == 1)
    def _():
        sa, sd = scores(qi - 1, False), scores(qi, True)
        absorb(qi - 1, sa, False)
        absorb(qi, sd, True)

    ot = acc_ref[...] / l_ref[...]
    lam = _lambda_value(lq1_ref, lk1_ref, lq2_ref, lk2_ref, lam_init)
    dt = ot[:, 0:tq] - lam * ot[:, tq:2 * tq]
    yt = dt * lax.rsqrt(jnp.mean(dt * dt, axis=0, keepdims=True) + NORM_EPS)
    o_ref[...] = (yt.T * g_ref[...] * (1.0 - lam_init)).astype(o_ref.dtype)


def _diff_attn_prompt(q, k, v4, lam_params, gain, *, bsz, seq, tq, lam_init):
    m = bsz * seq
    width = k.shape[1]
    heads = width // LANES
    nq = seq // tq
    small = lambda a: pl.BlockSpec(a.shape, lambda b, h, i: (0, 0))
    lam_args = [a.reshape(1, -1) for a in lam_params]
    gain = gain.reshape(1, -1)
    return pl.pallas_call(
        functools.partial(_diff_attn_kernel, lam_init=lam_init),
        grid=(bsz, heads, nq),
        in_specs=[pl.BlockSpec((tq, LANES), lambda b, h, i: (b * nq + i, h)),
                  pl.BlockSpec((seq, LANES), lambda b, h, i: (b, h)),
                  pl.BlockSpec((None, None, seq, LANES), lambda b, h, i: (b, h, 0, 0))]
                 + [small(a) for a in lam_args] + [small(gain)],
        out_specs=pl.BlockSpec((tq, LANES), lambda b, h, i: (b * nq + i, h)),
        out_shape=jax.ShapeDtypeStruct((m, width), BF16),
        scratch_shapes=[pltpu.VMEM((2 * tq, LANES), BF16),
                        pltpu.VMEM((seq // tq, LANES, tq), BF16),
                        pltpu.VMEM((1, 2 * tq), F32),
                        pltpu.VMEM((1, 2 * tq), F32),
                        pltpu.VMEM((LANES, 2 * tq), F32)],
        compiler_params=_params("parallel", "parallel", "arbitrary"),
    )(q, k, v4, *lam_args, gain)


def _decode_attn_kernel(pt_ref, q_ref, kn_ref, vn_ref, *refs, n_pages, lam_init):
    del pt_ref
    k_refs, v_refs = refs[:n_pages], refs[n_pages:2 * n_pages]
    lq1_ref, lk1_ref, lq2_ref, lk2_ref, g_ref, o_ref, qbd_ref, m_ref, l_ref, acc_ref = refs[2 * n_pages:]
    g = pl.program_id(1)
    width = q_ref.shape[1]

    @pl.when(g == 0)
    def _():
        row = lax.broadcasted_iota(jnp.int32, (DECODE_ROWS, width), 0)
        lane = lax.broadcasted_iota(jnp.int32, (DECODE_ROWS, width), 1)
        qb = jnp.broadcast_to(q_ref[...], (DECODE_ROWS, width))
        keep = (lane >= row * DIFF_HEAD_DIM) & (lane < (row + 1) * DIFF_HEAD_DIM)
        qbd_ref[...] = jnp.where(keep, qb, 0.0).astype(BF16)
        m_ref[...] = jnp.full(m_ref.shape, -jnp.inf, F32)
        l_ref[...] = jnp.zeros(l_ref.shape, F32)
        acc_ref[...] = jnp.zeros(acc_ref.shape, F32)

    qbd = qbd_ref[...]
    s = jnp.concatenate(
        [jnp.dot(qbd, kr[...].astype(BF16), preferred_element_type=F32) for kr in k_refs], axis=1)
    m_prev = m_ref[:, 0:1]
    m_next = jnp.maximum(m_prev, jnp.max(s, axis=-1, keepdims=True))
    alpha = jnp.exp(m_prev - m_next)
    p = jnp.exp(s - m_next)
    l_next = alpha * l_ref[:, 0:1] + jnp.sum(p, axis=-1, keepdims=True)
    pb = p.astype(BF16)
    page = k_refs[0].shape[1]
    for h in range(width // LANES):
        hs = slice(h * LANES, (h + 1) * LANES)
        pv = jnp.dot(pb[:, 0:page], v_refs[0][h].astype(BF16), preferred_element_type=F32)
        for r in range(1, n_pages):
            pv = pv + jnp.dot(pb[:, r * page:(r + 1) * page], v_refs[r][h].astype(BF16),
                              preferred_element_type=F32)
        acc_ref[:, hs] = alpha * acc_ref[:, hs] + pv
    m_ref[...] = jnp.broadcast_to(m_next, m_ref.shape)
    l_ref[...] = jnp.broadcast_to(l_next, l_ref.shape)

    @pl.when(g == pl.num_programs(1) - 1)
    def _():
        kn = kn_ref[...].astype(BF16).astype(F32)
        s_new = jnp.sum(qbd_ref[...].astype(F32) * kn, axis=-1, keepdims=True)
        m_old = m_ref[:, 0:1]
        m_fin = jnp.maximum(m_old, s_new)
        a = jnp.exp(m_old - m_fin)
        p_new = jnp.exp(s_new - m_fin)
        l_fin = a * l_ref[:, 0:1] + p_new
        o = (a * acc_ref[...] + p_new * vn_ref[...]) / l_fin
        lam = _lambda_value(lq1_ref, lk1_ref, lq2_ref, lk2_ref, lam_init)
        gain = g_ref[...]
        for h in range(width // LANES):
            hs = slice(h * LANES, (h + 1) * LANES)
            d = o[2 * h:2 * h + 1, hs] - lam * o[2 * h + 1:2 * h + 2, hs]
            o_ref[:, hs] = _subln(d, gain, lam_init).astype(o_ref.dtype)


def _decode_attn(q, k_new, v_new, cache_k, cache_v, page_table, lam_params, gain, *, lam_init):
    nb, _, width = q.shape
    page = cache_k.shape[2]
    n_used = page_table.shape[1]
    npg = DECODE_PAGES_PER_STEP
    assert n_used % npg == 0
    row = pl.BlockSpec((None, 1, width), lambda b, g, pt: (b, 0, 0))
    small = lambda a: pl.BlockSpec(a.shape, lambda b, g, pt: (0, 0))
    k_spec = lambda r: pl.BlockSpec((None, width, page),
                                    lambda b, g, pt: (pt[b, g * npg + r], 0, 0))
    v_spec = lambda r: pl.BlockSpec((None, width // LANES, page, LANES),
                                    lambda b, g, pt: (pt[b, g * npg + r], 0, 0, 0))
    lam_args = [a.reshape(1, -1) for a in lam_params]
    gain = gain.reshape(1, -1)
    grid_spec = pltpu.PrefetchScalarGridSpec(
        num_scalar_prefetch=1,
        grid=(nb, n_used // npg),
        in_specs=[row, row, row] + [k_spec(r) for r in range(npg)] + [v_spec(r) for r in range(npg)]
                 + [small(a) for a in lam_args] + [small(gain)],
        out_specs=pl.BlockSpec((None, 1, width), lambda b, g, pt: (b, 0, 0)),
        scratch_shapes=[pltpu.VMEM((DECODE_ROWS, width), BF16),
                        pltpu.VMEM((DECODE_ROWS, LANES), F32),
                        pltpu.VMEM((DECODE_ROWS, LANES), F32),
                        pltpu.VMEM((DECODE_ROWS, width), F32)],
    )
    return pl.pallas_call(
        functools.partial(_decode_attn_kernel, n_pages=npg, lam_init=lam_init),
        grid_spec=grid_spec,
        out_shape=jax.ShapeDtypeStruct((nb, 1, width), BF16),
        compiler_params=_params("parallel", "arbitrary"),
    )(page_table, q, k_new, v_new, *([cache_k] * npg), *([cache_v] * npg), *lam_args, gain)


def _rope_tables(pos):
    half = DIFF_HEAD_DIM // 2
    inv_freq = ROPE_THETA ** (-jnp.arange(half, dtype=F32) / half)
    ang = pos.astype(F32)[:, None] * inv_freq[None, :]
    cos, sin, zero = jnp.cos(ang), jnp.sin(ang), jnp.zeros_like(ang)
    reps = LANES // DIFF_HEAD_DIM
    cos_t = jnp.tile(jnp.concatenate([cos, cos], axis=1), (1, reps))
    sin_lo = jnp.tile(jnp.concatenate([-sin, zero], axis=1), (1, reps))
    sin_hi = jnp.tile(jnp.concatenate([zero, sin], axis=1), (1, reps))
    return cos_t, sin_lo, sin_hi


def _trunk(x, p, wb, side_plan, *, bsz, seq, tiles, rope, mem_k, mem_v, conv_state, cache, depth):
    m, d = x.shape
    n_a = depth // 2
    prompt = cache is None
    cdim = p['conv_w'].shape[-1]
    qk_width = p['qk_width']
    mem_width = MEM_HEADS * MEM_HEAD_DIM
    tm = tiles.proj
    act = BF16 if prompt else F32
    new_conv = []
    k_out = v_out = k_b = v_b = None

    def ffn(name, l, x, final_gain=None):
        jobs = side_plan.get((name, l), [])
        y, copies = _ffn(x, p['norm_' + name][l], wb[('w_' + name + '_up', l)], wb[('w_' + name + '_down', l)],
                         tm=tiles.ffn, tf=tiles.ffn_cols, final_gain=final_gain,
                         side=[(w3, layer) for _, w3, layer in jobs])
        for (key, _, _), copy in zip(jobs, copies):
            wb[key] = copy
        return y

    for l in range(depth):
        if l == n_a:
            if prompt:
                k_out, k_b, v_out, v_b = _norm_proj(
                    x, p['norm_kv'], wb[('w_kv', 0)],
                    [_Segment(qk_width, (_Out(F32, 'cols'), _Out(BF16)), rope=True),
                     _Segment(qk_width, (_Out(F32, 'heads'), _Out(BF16, 'heads')))],
                    tm=tm, seq=seq, rope=rope)
            else:
                k_out, v_out = _norm_proj(
                    x, p['norm_kv'], wb[('w_kv', 0)],
                    [_Segment(qk_width, (_Out(F32),), rope=True), _Segment(qk_width, (_Out(F32),))],
                    tm=tm, seq=seq, rope=rope)
        x = ffn('ffn1', l, x)
        if l < n_a:
            if prompt:
                mix, q_mem, st = _conv_mixer_prompt(x, p['norm_mix'][l], wb[('w_in_a', l)], p['conv_w'][l],
                                                    bsz=bsz, tm=tm, cdim=cdim)
            else:
                proj, q_mem = _norm_proj(x, p['norm_mix'][l], wb[('w_in_a', l)],
                                         [_Segment(3 * cdim, (_Out(act),)), _Segment(mem_width, (_Out(act),))],
                                         tm=tm, seq=seq)
                s0, s1 = conv_state[l]
                mix, u = _conv_gate_step(proj, s0, s1, p['conv_w'][l], cdim=cdim)
                st = jnp.stack([s1, u], axis=1)
            new_conv.append(st)
        else:
            j = l - n_a
            lam_init = 0.8 - 0.6 * math.exp(-0.3 * l)
            lam_params = (p['lambda_q1'][j], p['lambda_k1'][j], p['lambda_q2'][j], p['lambda_k2'][j])
            q_scale = DIFF_HEAD_DIM ** -0.5 * (LOG2E if prompt else 1.0)
            q, q_mem = _norm_proj(
                x, p['norm_mix'][l], wb[('w_in_b', j)],
                [_Segment(qk_width, (_Out(act),), rope=True, scale=q_scale),
                 _Segment(mem_width, (_Out(act),))],
                tm=tm, seq=seq, rope=rope)
            if prompt:
                mix = _diff_attn_prompt(q, k_b, v_b, lam_params, p['subln_gain'][j],
                                        bsz=bsz, seq=seq, tq=tiles.attn, lam_init=lam_init)
            else:
                cache_k, cache_v, page_table = cache
                mix = _decode_attn(q.reshape(bsz, 1, qk_width),
                                   k_out.reshape(bsz, 1, qk_width), v_out.reshape(bsz, 1, qk_width),
                                   cache_k, cache_v, page_table, lam_params, p['subln_gain'][j],
                                   lam_init=lam_init).reshape(m, qk_width)
        if prompt:
            x = _out_proj(mix, q_mem, wb[('w_out', l)], x, tm=tm, mem=(mem_k[l], mem_v[l], seq))
        else:
            mem_o = _mem_attn(q_mem.reshape(bsz, seq, mem_width), mem_k[l], mem_v[l], tq=seq)
            x = _out_proj(mix, mem_o.reshape(m, mem_width), wb[('w_out', l)], x, tm=tm)
        x = ffn('ffn2', l, x, final_gain=p['norm_final'] if l == depth - 1 else None)
    return x, new_conv, k_out, v_out


def kernel(x_prompt, x_sample, state_conv, cache_k, cache_v, cache_mem_k, cache_mem_v, page_table, mem_prompt, norm_ffn1, w_ffn1_up, w_ffn1_down, norm_mix, w_in_a, conv_w, w_in_b, lambda_q1, lambda_k1, lambda_q2, lambda_k2, subln_gain, norm_mem, w_mem_kv, w_out, norm_ffn2, w_ffn2_up, w_ffn2_down, norm_kv, w_kv, norm_final):
    bsz_p, s_p, d = x_prompt.shape
    bsz_s, s_s, _ = x_sample.shape
    depth = norm_ffn1.shape[0]
    n_mem = mem_prompt.shape[1]
    mem_width = MEM_HEADS * MEM_HEAD_DIM
    n_pool, page = cache_k.shape[0], cache_k.shape[1]
    k_heads, v_heads = cache_k.shape[2], cache_v.shape[2]
    qk_width = k_heads * cache_k.shape[3]
    past_len = page_table.shape[1] * page
    assert s_s == 1

    n_a = depth // 2
    p = {
        'norm_ffn1': norm_ffn1, 'norm_ffn2': norm_ffn2, 'norm_mix': norm_mix, 'conv_w': conv_w,
        'lambda_q1': lambda_q1, 'lambda_k1': lambda_k1, 'lambda_q2': lambda_q2, 'lambda_k2': lambda_k2,
        'subln_gain': subln_gain, 'norm_kv': norm_kv, 'norm_final': norm_final, 'qk_width': qk_width,
    }
    stacked = {'w_ffn1_up': w_ffn1_up, 'w_ffn1_down': w_ffn1_down, 'w_ffn2_up': w_ffn2_up,
               'w_ffn2_down': w_ffn2_down, 'w_in_a': w_in_a, 'w_in_b': w_in_b, 'w_out': w_out,
               'w_kv': w_kv[None]}

    def job(name, layer):
        return ((name, layer), stacked[name], layer)

    first = [job('w_ffn1_up', 0), job('w_ffn1_down', 0)] + ([job('w_kv', 0)] if n_a == 0 else [])
    wb = {key: _cast_layer(w3, layer) for key, w3, layer in first}
    side_plan = {}
    for l in range(depth):
        mixer = job('w_in_a', l) if l < n_a else job('w_in_b', l - n_a)
        side_plan[('ffn1', l)] = [mixer, job('w_out', l), job('w_ffn2_up', l), job('w_ffn2_down', l)]
        if l + 1 < depth:
            shared_kv = [job('w_kv', 0)] if l + 1 == n_a else []
            side_plan[('ffn2', l)] = shared_kv + [job('w_ffn1_up', l + 1), job('w_ffn1_down', l + 1)]

    mem_rows = mem_prompt.reshape(bsz_p * n_mem, d)
    mem_kv = [_norm_proj(mem_rows, norm_mem[l], w_mem_kv,
                         [_Segment(mem_width, (_Out(F32),)), _Segment(mem_width, (_Out(F32),))],
                         tm=bsz_p * n_mem, seq=n_mem, layer=l)
              for l in range(depth)]
    mem_k_p = [kv[0].reshape(bsz_p, n_mem, mem_width) for kv in mem_kv]
    mem_v_p = [kv[1].reshape(bsz_p, n_mem, mem_width) for kv in mem_kv]
    y_p, conv_p, k_t, v_h = _trunk(
        x_prompt.reshape(bsz_p * s_p, d), p, wb, side_plan, bsz=bsz_p, seq=s_p, tiles=PROMPT_TILES,
        rope=_rope_tables(jnp.arange(s_p)), mem_k=mem_k_p, mem_v=mem_v_p,
        conv_state=None, cache=None, depth=depth)

    rows_s = bsz_s * s_s
    pos_s = jnp.full((rows_s,), past_len, jnp.int32)
    cmk = cache_mem_k.reshape(depth, bsz_s, n_mem, mem_width)
    cmv = cache_mem_v.reshape(depth, bsz_s, n_mem, mem_width)
    y_s, conv_s, k_s, v_s = _trunk(
        x_sample.reshape(rows_s, d), p, wb, {}, bsz=bsz_s, seq=s_s,
        tiles=_Tiles(proj=rows_s, ffn=rows_s, ffn_cols=PROMPT_TILES.ffn_cols, attn=0),
        rope=_rope_tables(pos_s),
        mem_k=[cmk[l] for l in range(depth)], mem_v=[cmv[l] for l in range(depth)],
        conv_state=[(state_conv[l, :, 0], state_conv[l, :, 1]) for l in range(n_a)],
        cache=(jnp.transpose(cache_k, (0, 2, 3, 1)).reshape(n_pool, qk_width, page),
               jnp.transpose(cache_v, (0, 2, 1, 3)), page_table),
        depth=depth)

    mem_shape = (depth, bsz_p, n_mem, MEM_HEADS, MEM_HEAD_DIM)
    return (y_p.reshape(bsz_p, s_p, d),
            y_s.reshape(bsz_s, s_s, d),
            jnp.stack(conv_p, axis=0),
            jnp.stack(conv_s, axis=0),
            jnp.transpose(k_t.reshape(bsz_p, k_heads, -1, s_p), (0, 3, 1, 2)),
            jnp.transpose(v_h, (0, 2, 1, 3)),
            k_s.reshape(bsz_s, s_s, k_heads, -1),
            v_s.reshape(bsz_s, s_s, v_heads, -1),
            jnp.stack(mem_k_p, axis=0).reshape(mem_shape),
            jnp.stack(mem_v_p, axis=0).reshape(mem_shape))
```
